```python
import jax, jax.numpy as jnp
from jax import lax
import numpy as np

D_MODEL = 2048
BATCH = 8
SEQ = 2048
DEPTH = 2
DEC_BATCH = 128
DEC_SEQ = 1
PAST_LEN = 2048
PAGE_SIZE = 128

N_MIXERS = 2
N_RWKV_LAYERS = (DEPTH + 1) // 2
N_SB_LAYERS = DEPTH // 2
RWKV_HEAD = 64
RWKV_HEADS = D_MODEL // RWKV_HEAD
DECAY_LORA = 96
AAA_LORA = 96
GATE_LORA = 256
GN_EPS = 64e-5
SB_HEADS = 16
SB_HEAD_DIM = D_MODEL // SB_HEADS
Q_BLOCK = 128
SB_BIAS_INIT = -6.5
N_GROUPS = 4
EXPERTS_PER_GROUP = 8
N_EXPERTS = N_GROUPS * EXPERTS_PER_GROUP
TOP_K = 2
D_EXPERT = 256
RMS_EPS = 1e-6

kernel_name = "rwkv7_stickbreaking_hmoe_step"


def rmsnorm(x, g):
    xf = x.astype(jnp.float32)
    y = xf * lax.rsqrt(jnp.mean(xf * xf, axis=-1, keepdims=True) + RMS_EPS)
    return (y * g.astype(jnp.float32)).astype(x.dtype)


def rwkv7_mix(x, x_prev, wkv0, mu, w0, w1, w2, a0, a1, a2, g1, g2, k_k, k_a, r_k,
              wr, wk, wv, wo, ln_w, ln_b):
    f32 = jnp.float32
    B, T, D = x.shape
    H, N = RWKV_HEADS, RWKV_HEAD
    xx = jnp.concatenate([x_prev[:, None, :].astype(x.dtype), x[:, :-1]], axis=1) - x
    xr, xw, xk, xv, xa, xg = [x + xx * mu[i] for i in range(6)]
    r = xr @ wr
    k = xk @ wk
    v = xv @ wv
    w = -jax.nn.softplus(-(w0 + jnp.tanh(xw @ w1) @ w2)) - 0.5
    a = jax.nn.sigmoid(a0 + (xa @ a1) @ a2)
    g = jax.nn.sigmoid(xg @ g1) @ g2
    kk = (k * k_k).reshape(B, T, H, N).astype(f32)
    kk = kk / jnp.maximum(jnp.sqrt(jnp.sum(kk * kk, axis=-1, keepdims=True)), 1e-12)
    k = k * (1 + (a - 1) * k_a)
    heads = lambda t: t.reshape(B, T, H, N).astype(f32)
    rh, kh, vh, ah = heads(r), heads(k), heads(v), heads(a)
    decay = jnp.exp(-jnp.exp(heads(w)))

    def step(S, inp):
        r_t, d_t, k_t, v_t, kk_t, a_t = inp
        s_kk = jnp.einsum('bhij,bhj->bhi', S, kk_t)
        S = (S * d_t[:, :, None, :]
             - s_kk[..., None] * (kk_t * a_t)[:, :, None, :]
             + v_t[..., None] * k_t[:, :, None, :])
        return S, jnp.einsum('bhij,bhj->bhi', S, r_t)

    seq = tuple(jnp.swapaxes(t, 0, 1) for t in (rh, decay, kh, vh, kk, ah))
    wkv_T, y = lax.scan(step, wkv0.astype(f32), seq)
    y = jnp.swapaxes(y, 0, 1)
    mean = jnp.mean(y, axis=-1, keepdims=True)
    var = jnp.mean(jnp.square(y - mean), axis=-1, keepdims=True)
    y = ((y - mean) * lax.rsqrt(var + GN_EPS)).reshape(B, T, D) * ln_w.astype(f32) + ln_b.astype(f32)
    bonus = (jnp.sum(rh * kh * r_k.astype(f32), axis=-1, keepdims=True) * vh).reshape(B, T, D)
    out = ((y + bonus) * g.astype(f32)).astype(x.dtype) @ wo
    return out.astype(x.dtype), x[:, -1], wkv_T


def sb_weights(z, mask):
    log_not = jnp.where(mask, jax.nn.log_sigmoid(-z), 0.0)
    after = lax.cumsum(log_not, axis=z.ndim - 1, reverse=True) - log_not
    return jnp.where(mask, jnp.exp(jax.nn.log_sigmoid(z) + after), 0.0)


def sb_prompt(q, k, v, bias):
    B, T, H, Dh = q.shape
    scale = Dh ** -0.5
    b = bias.astype(jnp.float32)[None, :, None, None]
    outs = []
    for blk in range(T // Q_BLOCK):
        q0 = blk * Q_BLOCK
        end = q0 + Q_BLOCK
        z = jnp.einsum('bqhd,bshd->bhqs', q[:, q0:end], k[:, :end]).astype(jnp.float32) * scale + b
        mask = jnp.arange(end)[None, :] < (q0 + jnp.arange(Q_BLOCK))[:, None]
        A = sb_weights(z, mask)
        outs.append(jnp.einsum('bhqs,bshd->bqhd', A.astype(v.dtype), v[:, :end]))
    return jnp.concatenate(outs, axis=1)


def sb_sample(q, k_new, v_new, cache_k, cache_v, page_table, bias):
    T = q.shape[1]
    n_past = page_table.shape[1] * PAGE_SIZE
    scale = SB_HEAD_DIM ** -0.5
    b = bias.astype(jnp.float32)[:, None, None]
    mask = jnp.arange(n_past + T)[None, :] < (n_past + jnp.arange(T))[:, None]

    def one(args):
        q_s, kn, vn, pt = args
        kp = cache_k[pt].reshape(n_past, SB_HEADS, SB_HEAD_DIM).astype(kn.dtype)
        vp = cache_v[pt].reshape(n_past, SB_HEADS, SB_HEAD_DIM).astype(vn.dtype)
        ks = jnp.concatenate([kp, kn], axis=0)
        vs = jnp.concatenate([vp, vn], axis=0)
        z = jnp.einsum('qhd,shd->hqs', q_s, ks).astype(jnp.float32) * scale + b
        A = sb_weights(z, mask)
        return jnp.einsum('hqs,shd->qhd', A.astype(vs.dtype), vs)

    return lax.map(one, (q, k_new, v_new, page_table))


def hmoe(x, w_group, b_group, w_expert, b_expert, w_gate, w_up, w_down):
    f32 = jnp.float32
    B, T, D = x.shape
    xt = x.reshape(B * T, D)
    glog = (xt @ w_group).astype(f32) + b_group.astype(f32)
    gprob = jax.nn.softmax(glog, axis=-1)
    g_sel = jnp.argmax(glog, axis=-1)
    p_group = jnp.take_along_axis(gprob, g_sel[:, None], axis=1)
    elog = ((xt @ w_expert).astype(f32) + b_expert.astype(f32)).reshape(-1, N_GROUPS, EXPERTS_PER_GROUP)
    elog_sel = jnp.take_along_axis(elog, g_sel[:, None, None], axis=1)[:, 0]
    top_val, top_idx = lax.top_k(elog_sel, TOP_K)
    top_w = jax.nn.softmax(top_val, axis=-1) * p_group
    within = jnp.sum(jax.nn.one_hot(top_idx, EXPERTS_PER_GROUP, dtype=f32) * top_w[..., None], axis=1)
    gate = (jax.nn.one_hot(g_sel, N_GROUPS, dtype=f32)[:, :, None] * within[:, None, :]).reshape(-1, N_EXPERTS)
    h = jax.nn.silu(jnp.einsum('nd,edf->nef', xt, w_gate)) * jnp.einsum('nd,edf->nef', xt, w_up)
    y = jnp.einsum('nef,efd->nd', h * gate[:, :, None].astype(h.dtype), w_down)
    return y.reshape(B, T, D).astype(x.dtype)


def setup_inputs(seed: int = 0) -> dict:
    key = jax.random.key(seed)
    ks = iter(jax.random.split(key, 48))
    f32 = jnp.float32
    D, H, N = D_MODEL, RWKV_HEADS, RWKV_HEAD
    A_, S_ = N_RWKV_LAYERS, N_SB_LAYERS

    def nrm(shape, scale):
        return scale * jax.random.normal(next(ks), shape, f32)

    def unif(shape, lo, hi):
        return jax.random.uniform(next(ks), shape, f32, minval=lo, maxval=hi)

    n_pages = PAST_LEN // PAGE_SIZE
    n_used = DEC_BATCH * n_pages
    n_phys = n_used + max(1, n_used // 4)
    page_table = jax.random.permutation(next(ks), n_phys)[:n_used].reshape(DEC_BATCH, n_pages).astype(jnp.int32)
    return {
        "x_prompt": nrm((BATCH, SEQ, D), 1.0),
        "x_sample": nrm((DEC_BATCH, DEC_SEQ, D), 1.0),
        "state_tshift": nrm((A_, DEC_BATCH, D), 1.0),
        "state_wkv": nrm((A_, DEC_BATCH, H, N, N), 0.5),
        "cache_k": nrm((S_, n_phys, PAGE_SIZE, SB_HEADS, SB_HEAD_DIM), 1.0),
        "cache_v": nrm((S_, n_phys, PAGE_SIZE, SB_HEADS, SB_HEAD_DIM), 1.0),
        "page_table": page_table,
        "norm_mix": 1.0 + nrm((DEPTH, D), 0.05),
        "norm_ffn": 1.0 + nrm((DEPTH, D), 0.05),
        "norm_final": 1.0 + nrm((D,), 0.05),
        "rwkv_mu": unif((A_, 6, D), 0.0, 1.0),
        "rwkv_w0": unif((A_, D), -6.0, -1.0),
        "rwkv_w1": nrm((A_, D, DECAY_LORA), D ** -0.5),
        "rwkv_w2": nrm((A_, DECAY_LORA, D), 0.1 * DECAY_LORA ** -0.5),
        "rwkv_a0": nrm((A_, D), 0.1),
        "rwkv_a1": nrm((A_, D, AAA_LORA), D ** -0.5),
        "rwkv_a2": nrm((A_, AAA_LORA, D), 0.1 * AAA_LORA ** -0.5),
        "rwkv_g1": nrm((A_, D, GATE_LORA), D ** -0.5),
        "rwkv_g2": nrm((A_, GATE_LORA, D), GATE_LORA ** -0.5),
        "rwkv_kk": 0.85 + nrm((A_, D), 0.05),
        "rwkv_ka": 1.0 + nrm((A_, D), 0.05),
        "rwkv_rk": nrm((A_, H, N), 0.1),
        "rwkv_wr": nrm((A_, D, D), D ** -0.5),
        "rwkv_wk": nrm((A_, D, D), D ** -0.5),
        "rwkv_wv": nrm((A_, D, D), D ** -0.5),
        "rwkv_wo": nrm((A_, D, D), D ** -0.5),
        "rwkv_lnw": 1.0 + nrm((A_, D), 0.05),
        "rwkv_lnb": nrm((A_, D), 0.01),
        "sb_wqkv": nrm((S_, D, 3 * D), D ** -0.5),
        "sb_wo": nrm((S_, D, D), D ** -0.5),
        "sb_bias": SB_BIAS_INIT + nrm((S_, SB_HEADS), 0.1),
        "moe_wgroup": nrm((DEPTH, D, N_GROUPS), D ** -0.5),
        "moe_bgroup": nrm((DEPTH, N_GROUPS), 0.01),
        "moe_wexpert": nrm((DEPTH, D, N_EXPERTS), D ** -0.5),
        "moe_bexpert": nrm((DEPTH, N_EXPERTS), 0.01),
        "moe_wgate": nrm((DEPTH, N_EXPERTS, D, D_EXPERT), D ** -0.5),
        "moe_wup": nrm((DEPTH, N_EXPERTS, D, D_EXPERT), D ** -0.5),
        "moe_wdown": nrm((DEPTH, N_EXPERTS, D_EXPERT, D), D_EXPERT ** -0.5),
    }


def reference(x_prompt, x_sample, state_tshift, state_wkv, cache_k, cache_v, page_table,
              norm_mix, norm_ffn, norm_final,
              rwkv_mu, rwkv_w0, rwkv_w1, rwkv_w2, rwkv_a0, rwkv_a1, rwkv_a2, rwkv_g1, rwkv_g2,
              rwkv_kk, rwkv_ka, rwkv_rk, rwkv_wr, rwkv_wk, rwkv_wv, rwkv_wo, rwkv_lnw, rwkv_lnb,
              sb_wqkv, sb_wo, sb_bias,
              moe_wgroup, moe_bgroup, moe_wexpert, moe_bexpert, moe_wgate, moe_wup, moe_wdown):
    xp, xs = x_prompt, x_sample
    Bp, Tp, _ = xp.shape
    Bs, Ts, _ = xs.shape
    shift_p, wkv_p, k_p, v_p = [], [], [], []
    shift_s, wkv_s, k_s, v_s = [], [], [], []
    for i in range(DEPTH):
        hp = rmsnorm(xp, norm_mix[i])
        hs = rmsnorm(xs, norm_mix[i])
        j = i // N_MIXERS
        if i % N_MIXERS == 0:
            prm = (rwkv_mu[j], rwkv_w0[j], rwkv_w1[j], rwkv_w2[j], rwkv_a0[j], rwkv_a1[j], rwkv_a2[j],
                   rwkv_g1[j], rwkv_g2[j], rwkv_kk[j], rwkv_ka[j], rwkv_rk[j],
                   rwkv_wr[j], rwkv_wk[j], rwkv_wv[j], rwkv_wo[j], rwkv_lnw[j], rwkv_lnb[j])
            zero_shift = jnp.zeros((Bp, D_MODEL), xp.dtype)
            zero_wkv = jnp.zeros((Bp, RWKV_HEADS, RWKV_HEAD, RWKV_HEAD), jnp.float32)
            mp, lp, sp = rwkv7_mix(hp, zero_shift, zero_wkv, *prm)
            ms, ls, ss = rwkv7_mix(hs, state_tshift[j], state_wkv[j], *prm)
            shift_p.append(lp)
            wkv_p.append(sp.astype(xp.dtype))
            shift_s.append(ls.astype(state_tshift.dtype))
            wkv_s.append(ss.astype(state_wkv.dtype))
        else:
            qp, kp, vp = [t.reshape(Bp, Tp, SB_HEADS, SB_HEAD_DIM)
                          for t in jnp.split(hp @ sb_wqkv[j], 3, axis=-1)]
            qs, kss, vss = [t.reshape(Bs, Ts, SB_HEADS, SB_HEAD_DIM)
                            for t in jnp.split(hs @ sb_wqkv[j], 3, axis=-1)]
            mp = sb_prompt(qp, kp, vp, sb_bias[j]).reshape(Bp, Tp, D_MODEL) @ sb_wo[j]
            ms = sb_sample(qs, kss, vss, cache_k[j], cache_v[j], page_table,
                           sb_bias[j]).reshape(Bs, Ts, D_MODEL) @ sb_wo[j]
            k_p.append(kp)
            v_p.append(vp)
            k_s.append(kss)
            v_s.append(vss)
        xp = xp + mp
        xs = xs + ms
        moe_prm = (moe_wgroup[i], moe_bgroup[i], moe_wexpert[i], moe_bexpert[i],
                   moe_wgate[i], moe_wup[i], moe_wdown[i])
        xp = xp + hmoe(rmsnorm(xp, norm_ffn[i]), *moe_prm)
        xs = xs + hmoe(rmsnorm(xs, norm_ffn[i]), *moe_prm)
    y_prompt = rmsnorm(xp, norm_final)
    y_sample = rmsnorm(xs, norm_final)
    return (y_prompt, y_sample,
            jnp.stack(shift_p), jnp.stack(wkv_p), jnp.stack(k_p), jnp.stack(v_p),
            jnp.stack(shift_s), jnp.stack(wkv_s), jnp.stack(k_s), jnp.stack(v_s))
```

```python
import functools

import jax
import jax.numpy as jnp
from jax import lax
from jax.experimental import pallas as pl
from jax.experimental.pallas import tpu as pltpu

F32 = jnp.float32
BF16 = jnp.bfloat16

D_MODEL = 2048
RWKV_HEAD = 64
RWKV_HEADS = D_MODEL // RWKV_HEAD
HEAD_PAIRS = RWKV_HEADS // 2
LORA_PAD = 128
GN_EPS = 64e-5
RMS_EPS = 1e-6
SB_HEADS = 16
SB_HEAD_DIM = D_MODEL // SB_HEADS
Q_BLOCK = 128
PAGE_SIZE = 128
N_GROUPS = 4
EXPERTS_PER_GROUP = 8
N_EXPERTS = N_GROUPS * EXPERTS_PER_GROUP
D_EXPERT = 256
LANES = 128
SUBLANES = 8
VMEM_LIMIT_BYTES = 52 * 1024 * 1024


def _params(*semantics):
    return pltpu.CompilerParams(dimension_semantics=semantics, vmem_limit_bytes=VMEM_LIMIT_BYTES)


def _softplus(x):
    return jnp.maximum(x, 0.0) + jnp.log1p(jnp.exp(-jnp.abs(x)))


def _sigmoid(x):
    return 1.0 / (1.0 + jnp.exp(-x))


def _dot(a, b):
    return jnp.dot(a, b, preferred_element_type=F32)


def _dot_hilo(a, b_bf16):
    hi = a.astype(BF16)
    lo = (a - hi.astype(F32)).astype(BF16)
    return _dot(hi, b_bf16) + _dot(lo, b_bf16)


def _block_ones(n, seg):
    r = lax.broadcasted_iota(jnp.int32, (n, n), 0) // seg
    c = lax.broadcasted_iota(jnp.int32, (n, n), 1) // seg
    return jnp.where(r == c, 1.0, 0.0).astype(BF16)


def _rmsnorm_body(x_ref, g_ref, o_ref):
    x = x_ref[...]
    ms = jnp.mean(x * x, axis=-1, keepdims=True)
    o_ref[...] = x * lax.rsqrt(ms + RMS_EPS) * g_ref[...]


def _rmsnorm(x, g, tm):
    n, d = x.shape
    return pl.pallas_call(
        _rmsnorm_body,
        grid=(n // tm,),
        in_specs=[pl.BlockSpec((tm, d), lambda i: (i, 0)), pl.BlockSpec((1, d), lambda i: (0, 0))],
        out_specs=pl.BlockSpec((tm, d), lambda i: (i, 0)),
        out_shape=jax.ShapeDtypeStruct((n, d), F32),
        compiler_params=_params("parallel"),
        name="rmsnorm",
    )(x, g.reshape(1, d))


def _mm_body(x_ref, w_ref, o_ref):
    o_ref[...] = _dot(x_ref[...].astype(BF16), w_ref[...])


def _mm_res_body(x_ref, w_ref, r_ref, o_ref):
    o_ref[...] = r_ref[...] + _dot(x_ref[...].astype(BF16), w_ref[...])


def _matmul(x, w, tm, residual=None):
    n, k = x.shape
    m = w.shape[1]
    row = lambda i: (i, 0)
    in_specs = [pl.BlockSpec((tm, k), row), pl.BlockSpec((k, m), lambda i: (0, 0))]
    args = [x, w]
    body = _mm_body
    if residual is not None:
        in_specs.append(pl.BlockSpec((tm, m), row))
        args.append(residual)
        body = _mm_res_body
    return pl.pallas_call(
        body,
        grid=(n // tm,),
        in_specs=in_specs,
        out_specs=pl.BlockSpec((tm, m), row),
        out_shape=jax.ShapeDtypeStruct((n, m), F32),
        compiler_params=_params("parallel"),
        name="matmul",
    )(*args)


def _token_shift(h_ref, tail_ref, prev_ref, i, tm, seq_len):
    h = h_ref[...]
    if prev_ref is not None:
        return h, prev_ref[...]
    rolled = pltpu.roll(h, 1, 0)
    starts_sequence = (i * tm) % seq_len == 0
    first = jnp.where(starts_sequence, 0.0, tail_ref[SUBLANES - 1:SUBLANES, :])
    row = lax.broadcasted_iota(jnp.int32, h.shape, 0)
    return h, jnp.where(row == 0, first, rolled)


def _shift_specs(tm, d, has_prev, grid_rank):
    if grid_rank == 1:
        cur = lambda i: (i, 0)
        tail = lambda i: (jnp.maximum(i * (tm // SUBLANES) - 1, 0), 0)
    else:
        cur = lambda p, i: (i, 0)
        tail = lambda p, i: (jnp.maximum(i * (tm // SUBLANES) - 1, 0), 0)
    if has_prev:
        return [pl.BlockSpec((tm, d), cur), pl.BlockSpec((tm, d), cur)]
    return [pl.BlockSpec((tm, d), cur), pl.BlockSpec((SUBLANES, d), tail)]


def _rkv_body(h_ref, aux_ref, mu_ref, w_ref, o_ref, *, tm, seq_len, has_prev):
    i = pl.program_id(1)
    h, prev = _token_shift(h_ref, None if has_prev else aux_ref, aux_ref if has_prev else None, i, tm, seq_len)
    x = h + (prev - h) * mu_ref[0]
    o_ref[0] = _dot(x.astype(BF16), w_ref[0])


def _rwkv_rkv(h, prev, mu3, w3, tm, seq_len):
    n, d = h.shape
    has_prev = prev is not None
    return pl.pallas_call(
        functools.partial(_rkv_body, tm=tm, seq_len=seq_len, has_prev=has_prev),
        grid=(3, n // tm),
        in_specs=_shift_specs(tm, d, has_prev, 2) + [
            pl.BlockSpec((1, 1, d), lambda p, i: (p, 0, 0)),
            pl.BlockSpec((1, d, d), lambda p, i: (p, 0, 0)),
        ],
        out_specs=pl.BlockSpec((1, tm, d), lambda p, i: (p, i, 0)),
        out_shape=jax.ShapeDtypeStruct((3, n, d), F32),
        compiler_params=_params("arbitrary", "arbitrary"),
        name="rwkv_rkv",
    )(h, prev if has_prev else h, mu3, w3)


def _lora_body(h_ref, aux_ref, mu_ref, w0_ref, a0_ref, w1_ref, w2_ref, a1_ref, a2_ref, g1_ref, g2_ref,
               decay_ref, a_ref, g_ref, *, tm, seq_len, has_prev):
    i = pl.program_id(0)
    h, prev = _token_shift(h_ref, None if has_prev else aux_ref, aux_ref if has_prev else None, i, tm, seq_len)
    xx = prev - h
    xw = (h + xx * mu_ref[0]).astype(BF16)
    xa = (h + xx * mu_ref[1]).astype(BF16)
    xg = (h + xx * mu_ref[2]).astype(BF16)
    lw = _dot(jnp.tanh(_dot(xw, w1_ref[...])).astype(BF16), w2_ref[...])
    w = -_softplus(-(w0_ref[...] + lw)) - 0.5
    decay_ref[...] = jnp.exp(-jnp.exp(w))
    la = _dot(_dot(xa, a1_ref[...]).astype(BF16), a2_ref[...])
    a_ref[...] = _sigmoid(a0_ref[...] + la)
    g_ref[...] = _dot(_sigmoid(_dot(xg, g1_ref[...])).astype(BF16), g2_ref[...])


def _rwkv_lora(h, prev, mu3, w0, a0, w1, w2, a1, a2, g1, g2, tm, seq_len):
    n, d = h.shape
    has_prev = prev is not None
    full = lambda arr: pl.BlockSpec(arr.shape, lambda i, nd=arr.ndim: (0,) * nd)
    row = pl.BlockSpec((tm, d), lambda i: (i, 0))
    out = jax.ShapeDtypeStruct((n, d), F32)
    return pl.pallas_call(
        functools.partial(_lora_body, tm=tm, seq_len=seq_len, has_prev=has_prev),
        grid=(n // tm,),
        in_specs=_shift_specs(tm, d, has_prev, 1) + [full(mu3), full(w0), full(a0), full(w1), full(w2),
                                                     full(a1), full(a2), full(g1), full(g2)],
        out_specs=[row, row, row],
        out_shape=[out, out, out],
        compiler_params=_params("parallel"),
        name="rwkv_lora",
    )(h, prev if has_prev else h, mu3, w0, a0, w1, w2, a1, a2, g1, g2)


PAIR_GROUP = 4


def _rows8(x):
    if x.shape[0] >= SUBLANES:
        return x
    return jnp.broadcast_to(x[0:1], (SUBLANES, x.shape[1]))


def _segsum(x, ones_bd):
    m = x.shape[0]
    return _dot_hilo(_rows8(x), ones_bd)[:m]


def _wkv_body(r_ref, k_ref, v_ref, d_ref, a_ref, g_ref, kkw_ref, kaw_ref, rkw_ref, lnw_ref, lnb_ref, s0_ref,
              o_ref, st_ref,
              s_scr, kk_scr, kp_scr, ka_scr, dr_scr, c1_scr, c2_scr, dd_scr, vv_scr, y_scr, *, tc):
    t_chunk = pl.program_id(1)
    unroll = min(tc, SUBLANES)
    ones_bd = _block_ones(LANES, RWKV_HEAD)
    rows = lax.broadcasted_iota(jnp.int32, (RWKV_HEAD, LANES), 0)
    lanes = lax.broadcasted_iota(jnp.int32, (RWKV_HEAD, LANES), 1)
    diag = (lanes % RWKV_HEAD) == rows

    @pl.when(t_chunk == 0)
    def _load_state():
        for p in range(HEAD_PAIRS):
            s_scr[p] = jnp.concatenate([s0_ref[0, 2 * p], s0_ref[0, 2 * p + 1]], axis=1)

    for p in range(HEAD_PAIRS):
        sl = slice(p * LANES, (p + 1) * LANES)
        r = r_ref[0, :, sl]
        k = k_ref[0, :, sl]
        a = a_ref[0, :, sl]
        kk = k * kkw_ref[:, sl]
        norm = jnp.sqrt(_segsum(kk * kk, ones_bd))
        kk = kk / jnp.maximum(norm, 1e-12)
        kp = k * (1.0 + (a - 1.0) * kaw_ref[:, sl])
        ka = kk * a
        d = d_ref[0, :, sl]
        kk_scr[p] = _rows8(kk)
        kp_scr[p] = _rows8(kp)
        ka_scr[p] = _rows8(ka)
        dr_scr[p] = _rows8(d * r)
        c1_scr[p] = _rows8(_segsum(ka * r, ones_bd))
        c2_scr[p] = _rows8(_segsum(kp * r, ones_bd))
        dd_scr[p] = _rows8(d)
        vv_scr[p] = _rows8(v_ref[0, :, sl])

    blocks = tc // unroll

    def group_steps(idx, carry):
        grp = idx // blocks
        t0 = pl.multiple_of((idx % blocks) * unroll, unroll) if blocks > 1 else 0
        for pp in range(PAIR_GROUP):
            p = grp * PAIR_GROUP + pp
            rows_of = lambda ref: ref[p, pl.ds(t0, SUBLANES), :]
            kk8, dr8, ka8, kp8, c18, c28, dd8, vv8 = [rows_of(ref) for ref in (
                kk_scr, dr_scr, ka_scr, kp_scr, c1_scr, c2_scr, dd_scr, vv_scr)]
            s = s_scr[p]
            y_rows = []
            for j in range(unroll):
                row = lambda x: x[j:j + 1, :]
                s_kk = _dot_hilo(s * row(kk8), ones_bd)
                y_part = _dot_hilo(s * row(dr8), ones_bd)
                v_col = _dot_hilo(jnp.where(diag, row(vv8), 0.0), ones_bd)
                s = s * row(dd8) - s_kk * row(ka8) + v_col * row(kp8)
                y_col = y_part - s_kk * row(c18) + v_col * row(c28)
                y_rows.append(jnp.sum(jnp.where(diag, y_col, 0.0), axis=0, keepdims=True))
            s_scr[p] = s
            y_scr[p, pl.ds(t0, SUBLANES), :] = _rows8(jnp.concatenate(y_rows, axis=0))
        return carry
    lax.fori_loop(0, (HEAD_PAIRS // PAIR_GROUP) * blocks, group_steps, 0)

    for p in range(HEAD_PAIRS):
        sl = slice(p * LANES, (p + 1) * LANES)
        y = y_scr[p, 0:tc, :]
        mean = _segsum(y, ones_bd) * (1.0 / RWKV_HEAD)
        yc = y - mean
        var = _segsum(yc * yc, ones_bd) * (1.0 / RWKV_HEAD)
        yn = yc * lax.rsqrt(var + GN_EPS) * lnw_ref[:, sl] + lnb_ref[:, sl]
        bonus = _segsum(r_ref[0, :, sl] * kp_scr[p, 0:tc, :] * rkw_ref[:, sl], ones_bd) * v_ref[0, :, sl]
        o_ref[0, :, sl] = (yn + bonus) * g_ref[0, :, sl]

    @pl.when(t_chunk == pl.num_programs(1) - 1)
    def _store_state():
        for p in range(HEAD_PAIRS):
            s = s_scr[p]
            st_ref[0, 2 * p] = s[:, :RWKV_HEAD]
            st_ref[0, 2 * p + 1] = s[:, RWKV_HEAD:]


def _wkv(rkv, decay, a, g, kkw, kaw, rkw, lnw, lnb, s0, batch, seq_len, tc):
    d = D_MODEL
    rkv4 = rkv.reshape(3, batch, seq_len, d)
    seq3 = lambda x: x.reshape(batch, seq_len, d)
    tok = lambda which: pl.BlockSpec((None, 1, tc, d), lambda b, t, which=which: (which, b, t, 0))
    tok3 = pl.BlockSpec((1, tc, d), lambda b, t: (b, t, 0))
    vec = pl.BlockSpec((1, d), lambda b, t: (0, 0))
    state = pl.BlockSpec((1, RWKV_HEADS, RWKV_HEAD, RWKV_HEAD), lambda b, t: (b, 0, 0, 0))
    chunk = pltpu.VMEM((HEAD_PAIRS, max(tc, SUBLANES), LANES), F32)
    o, st = pl.pallas_call(
        functools.partial(_wkv_body, tc=tc),
        grid=(batch, seq_len // tc),
        in_specs=[tok(0), tok(1), tok(2), tok3, tok3, tok3, vec, vec, vec, vec, vec, state],
        out_specs=[tok3, state],
        out_shape=[jax.ShapeDtypeStruct((batch, seq_len, d), F32),
                   jax.ShapeDtypeStruct((batch, RWKV_HEADS, RWKV_HEAD, RWKV_HEAD), F32)],
        scratch_shapes=[pltpu.VMEM((HEAD_PAIRS, RWKV_HEAD, LANES), F32)] + [chunk] * 9,
        compiler_params=_params("parallel", "arbitrary"),
        name="wkv",
    )(rkv4, rkv4, rkv4, seq3(decay), seq3(a), seq3(g), kkw, kaw, rkw, lnw, lnb, s0)
    return o.reshape(batch * seq_len, d), st


def _sb_prompt_body(q_ref, k_ref, v_ref, b_ref, o_ref):
    qi = pl.program_id(2)
    scale = SB_HEAD_DIM ** -0.5
    q = q_ref[0].astype(BF16)
    bias = b_ref[0]
    t_idx = lax.broadcasted_iota(jnp.int32, (Q_BLOCK, Q_BLOCK), 0)
    s_idx = lax.broadcasted_iota(jnp.int32, (Q_BLOCK, Q_BLOCK), 1)
    later = jnp.where(t_idx > s_idx, 1.0, 0.0).astype(BF16)
    ones = jnp.ones((Q_BLOCK, Q_BLOCK), BF16)

    def body(j, carry):
        acc, tail = carry
        kb = qi - j
        start = pl.multiple_of(kb * Q_BLOCK, Q_BLOCK)
        kblk = k_ref[0, pl.ds(start, Q_BLOCK), :].astype(BF16)
        vblk = v_ref[0, pl.ds(start, Q_BLOCK), :].astype(BF16)
        z = lax.dot_general(q, kblk, (((1,), (1,)), ((), ())), preferred_element_type=F32) * scale + bias
        sp = _softplus(z)
        mask = (kb * Q_BLOCK + s_idx) < (qi * Q_BLOCK + t_idx)
        log_not = jnp.where(mask, -sp, 0.0)
        after = _dot_hilo(log_not, later) + tail
        p = jnp.where(mask, jnp.exp(z - sp + after), 0.0)
        acc = acc + _dot(p.astype(BF16), vblk)
        tail = tail + _dot_hilo(log_not, ones)
        return acc, tail

    zeros = jnp.zeros((Q_BLOCK, SB_HEAD_DIM), F32)
    acc, _ = lax.fori_loop(0, qi + 1, body, (zeros, zeros))
    o_ref[0] = acc


def _sb_prompt(qkv, bias_lanes, batch, seq_len):
    d = D_MODEL
    qkv4 = qkv.reshape(3, batch, seq_len, d)
    qspec = pl.BlockSpec((None, 1, Q_BLOCK, SB_HEAD_DIM), lambda b, h, i: (0, b, i, h))
    kspec = pl.BlockSpec((None, 1, seq_len, SB_HEAD_DIM), lambda b, h, i: (1, b, 0, h))
    vspec = pl.BlockSpec((None, 1, seq_len, SB_HEAD_DIM), lambda b, h, i: (2, b, 0, h))
    out = pl.pallas_call(
        _sb_prompt_body,
        grid=(batch, SB_HEADS, seq_len // Q_BLOCK),
        in_specs=[qspec, kspec, vspec, pl.BlockSpec((1, 1, LANES), lambda b, h, i: (h, 0, 0))],
        out_specs=pl.BlockSpec((1, Q_BLOCK, SB_HEAD_DIM), lambda b, h, i: (b, i, h)),
        out_shape=jax.ShapeDtypeStruct((batch, seq_len, d), F32),
        compiler_params=_params("parallel", "parallel", "arbitrary"),
        name="sb_prompt",
    )(qkv4, qkv4, qkv4, bias_lanes)
    return out.reshape(batch * seq_len, d)


def _sb_sample_body(pt_ref, q_ref, kn_ref, vn_ref, kc_ref, vc_ref, b_ref, hs_ref, o_ref, acc_scr, tail_scr):
    step = pl.program_id(1)
    n_steps = pl.num_programs(1)
    scale = SB_HEAD_DIM ** -0.5
    d = D_MODEL
    head_sel = hs_ref[...]
    q = q_ref[0]
    bias = b_ref[...]
    r_idx = lax.broadcasted_iota(jnp.int32, (PAGE_SIZE, PAGE_SIZE), 0)
    c_idx = lax.broadcasted_iota(jnp.int32, (PAGE_SIZE, PAGE_SIZE), 1)
    later = jnp.where(c_idx > r_idx, 1.0, 0.0).astype(BF16)

    @pl.when(step == 0)
    def _init():
        n_past = n_steps * PAGE_SIZE
        new_is_before = jnp.full((SUBLANES, LANES), n_past, jnp.int32) < n_past
        z_new = _dot(_rows8(q * kn_ref[0]).astype(BF16), head_sel) * scale + bias
        sp_new = _softplus(z_new)
        tail_scr[...] = jnp.where(new_is_before, -sp_new, 0.0)
        w_new = jnp.where(new_is_before, jnp.exp(z_new - sp_new), 0.0)
        w_cols = lax.dot_general(w_new.astype(BF16), head_sel, (((1,), (1,)), ((), ())),
                                 preferred_element_type=F32)
        acc_scr[...] = jnp.broadcast_to(w_cols[0:1] * vn_ref[0], acc_scr.shape)

    kp = kc_ref[0]
    z = _dot((kp * q).astype(BF16), head_sel) * scale + bias
    sp = _softplus(z)
    log_not = -sp
    hi = log_not.astype(BF16)
    lo = (log_not - hi.astype(F32)).astype(BF16)
    tail = tail_scr[0:1, :]
    after = _dot(later, hi) + _dot(later, lo) + tail
    p = jnp.exp(z - sp + after)
    tail_scr[...] = jnp.broadcast_to(tail + jnp.sum(log_not, axis=0, keepdims=True), tail_scr.shape)
    p_heads = p.T[:SB_HEADS, :].astype(BF16)
    acc_scr[...] += _dot(p_heads, vc_ref[0].astype(BF16))

    @pl.when(step == n_steps - 1)
    def _finish():
        hrow = lax.broadcasted_iota(jnp.int32, (SB_HEADS, d), 0)
        hcol = lax.broadcasted_iota(jnp.int32, (SB_HEADS, d), 1) // SB_HEAD_DIM
        o_ref[0] = jnp.sum(jnp.where(hrow == hcol, acc_scr[...], 0.0), axis=0, keepdims=True)


def _sb_sample(q, k_new, v_new, cache_k, cache_v, page_table, bias_row):
    batch, d = q.shape
    n_pages = page_table.shape[1]
    col_head = lax.broadcasted_iota(jnp.int32, (d, LANES), 0) // SB_HEAD_DIM
    head_sel = (col_head == lax.broadcasted_iota(jnp.int32, (d, LANES), 1)).astype(BF16)
    tok = pl.BlockSpec((1, 1, d), lambda b, s, pt: (b, 0, 0))
    page = pl.BlockSpec((1, PAGE_SIZE, d), lambda b, s, pt: (pt[b, n_pages - 1 - s], 0, 0))
    as3 = lambda x: x.reshape(batch, 1, d)
    out = pl.pallas_call(
        _sb_sample_body,
        grid_spec=pltpu.PrefetchScalarGridSpec(
            num_scalar_prefetch=1,
            grid=(batch, n_pages),
            in_specs=[tok, tok, tok, page, page, pl.BlockSpec((1, LANES), lambda b, s, pt: (0, 0)),
                      pl.BlockSpec((d, LANES), lambda b, s, pt: (0, 0))],
            out_specs=tok,
            scratch_shapes=[pltpu.VMEM((SB_HEADS, d), F32), pltpu.VMEM((SUBLANES, LANES), F32)],
        ),
        out_shape=jax.ShapeDtypeStruct((batch, 1, d), F32),
        compiler_params=_params("parallel", "arbitrary"),
        name="sb_sample",
    )(page_table, as3(q), as3(k_new), as3(v_new), cache_k, cache_v, bias_row, head_sel)
    return out.reshape(batch, d)


ROUTER_OFFSET = N_GROUPS


def _router_body(x_ref, w_ref, b_ref, gate_ref):
    logits = jnp.dot(x_ref[...], w_ref[...], preferred_element_type=F32,
                     precision=lax.Precision.HIGHEST) + b_ref[...]
    lane = lax.broadcasted_iota(jnp.int32, logits.shape, 1)
    neg = -jnp.inf
    first_lane_of = lambda hit: jnp.min(jnp.where(hit, lane, LANES), axis=-1, keepdims=True)
    is_group = lane < N_GROUPS
    glog = jnp.where(is_group, logits, neg)
    gmax = jnp.max(glog, axis=-1, keepdims=True)
    g_sel = first_lane_of(glog == gmax)
    p_group = 1.0 / jnp.sum(jnp.exp(glog - gmax), axis=-1, keepdims=True)
    in_group = ((lane - ROUTER_OFFSET) // EXPERTS_PER_GROUP == g_sel) & (lane >= ROUTER_OFFSET) \
        & (lane < ROUTER_OFFSET + N_EXPERTS)
    elog = jnp.where(in_group, logits, neg)
    top1 = jnp.max(elog, axis=-1, keepdims=True)
    lane1 = first_lane_of(elog == top1)
    rest = jnp.where(lane == lane1, neg, elog)
    top2 = jnp.max(rest, axis=-1, keepdims=True)
    lane2 = first_lane_of(rest == top2)
    e2 = jnp.exp(top2 - top1)
    w1 = p_group / (1.0 + e2)
    w2 = p_group * e2 / (1.0 + e2)
    gate_ref[...] = jnp.where(lane == lane1, w1, 0.0) + jnp.where(lane == lane2, w2, 0.0)


def _router(x, w_router, b_router, tm):
    n, d = x.shape
    return pl.pallas_call(
        _router_body,
        grid=(n // tm,),
        in_specs=[pl.BlockSpec((tm, d), lambda i: (i, 0)),
                  pl.BlockSpec((d, LANES), lambda i: (0, 0)),
                  pl.BlockSpec((1, LANES), lambda i: (0, 0))],
        out_specs=pl.BlockSpec((tm, LANES), lambda i: (i, 0)),
        out_shape=jax.ShapeDtypeStruct((n, LANES), F32),
        compiler_params=_params("parallel"),
        name="moe_router",
    )(x, w_router, b_router)


def _moe_body(x_ref, gate_ref, res_ref, wg_ref, wu_ref, wd_ref, o_ref, xb_scr):
    e = pl.program_id(1)

    @pl.when(e == 0)
    def _init():
        xb_scr[...] = x_ref[...].astype(BF16)
        o_ref[...] = res_ref[...]

    xb = xb_scr[...]
    hg = _dot(xb, wg_ref[0])
    hu = _dot(xb, wu_ref[0])
    gate = gate_ref[...]
    lane = lax.broadcasted_iota(jnp.int32, gate.shape, 1)
    ge = jnp.sum(jnp.where(lane == e + ROUTER_OFFSET, gate, 0.0), axis=-1, keepdims=True)
    act = (hg * _sigmoid(hg)) * hu * ge
    o_ref[...] += _dot(act.astype(BF16), wd_ref[0])


def _moe_dense(x, gate, residual, wg, wu, wd, tm):
    n, d = x.shape
    row = lambda i, e: (i, 0)
    return pl.pallas_call(
        _moe_body,
        grid=(n // tm, N_EXPERTS),
        in_specs=[pl.BlockSpec((tm, d), row), pl.BlockSpec((tm, LANES), row), pl.BlockSpec((tm, d), row),
                  pl.BlockSpec((1, d, D_EXPERT), lambda i, e: (e, 0, 0)),
                  pl.BlockSpec((1, d, D_EXPERT), lambda i, e: (e, 0, 0)),
                  pl.BlockSpec((1, D_EXPERT, d), lambda i, e: (e, 0, 0))],
        out_specs=pl.BlockSpec((tm, d), row),
        out_shape=jax.ShapeDtypeStruct((n, d), F32),
        scratch_shapes=[pltpu.VMEM((tm, d), BF16)],
        compiler_params=_params("parallel", "arbitrary"),
        name="moe_experts",
    )(x, gate, residual, wg, wu, wd)


def _row_tile(n):
    return 256 if n % 256 == 0 else LANES


def _ffn(x, norm_g, w_router, b_router, wg, wu, wd):
    tm = _row_tile(x.shape[0])
    h = _rmsnorm(x, norm_g, tm)
    gate = _router(h, w_router, b_router, tm)
    return _moe_dense(h, gate, x, wg, wu, wd, 512 if x.shape[0] % 512 == 0 else LANES)


def kernel(x_prompt, x_sample, state_tshift, state_wkv, cache_k, cache_v, page_table, norm_mix, norm_ffn, norm_final, rwkv_mu, rwkv_w0, rwkv_w1, rwkv_w2, rwkv_a0, rwkv_a1, rwkv_a2, rwkv_g1, rwkv_g2, rwkv_kk, rwkv_ka, rwkv_rk, rwkv_wr, rwkv_wk, rwkv_wv, rwkv_wo, rwkv_lnw, rwkv_lnb, sb_wqkv, sb_wo, sb_bias, moe_wgroup, moe_bgroup, moe_wexpert, moe_bexpert, moe_wgate, moe_wup, moe_wdown):
    d = D_MODEL
    bp, tp, _ = x_prompt.shape
    bs, ts, _ = x_sample.shape
    xp = x_prompt.reshape(bp * tp, d)
    xs = x_sample.reshape(bs * ts, d)
    row1 = lambda v: v.reshape(1, d)
    pad_cols = lambda w: jnp.pad(w, ((0, 0), (0, LORA_PAD - w.shape[1]))).astype(BF16)
    pad_rows = lambda w: jnp.pad(w, ((0, LORA_PAD - w.shape[0]), (0, 0))).astype(BF16)

    mu = rwkv_mu[0]
    mu_rkv = jnp.stack([mu[0], mu[2], mu[3]]).reshape(3, 1, d)
    mu_lora = jnp.stack([mu[1], mu[4], mu[5]]).reshape(3, 1, d)
    w_rkv = jnp.stack([rwkv_wr[0], rwkv_wk[0], rwkv_wv[0]]).astype(BF16)
    lora = (row1(rwkv_w0[0]), row1(rwkv_a0[0]), pad_cols(rwkv_w1[0]), pad_rows(rwkv_w2[0]),
            pad_cols(rwkv_a1[0]), pad_rows(rwkv_a2[0]), rwkv_g1[0].astype(BF16), rwkv_g2[0].astype(BF16))
    head_vecs = (row1(rwkv_kk[0]), row1(rwkv_ka[0]), row1(rwkv_rk[0]), row1(rwkv_lnw[0]), row1(rwkv_lnb[0]))
    wo = rwkv_wo[0].astype(BF16)

    def rwkv_layer(x, prev, s0, batch, seq_len, tc):
        tm = _row_tile(x.shape[0])
        h = _rmsnorm(x, norm_mix[0], tm)
        rkv = _rwkv_rkv(h, prev, mu_rkv, w_rkv, tm, seq_len)
        decay, a, g = _rwkv_lora(h, prev, mu_lora, *lora, tm, seq_len)
        o, s_final = _wkv(rkv, decay, a, g, *head_vecs, s0, batch, seq_len, tc)
        return _matmul(o, wo, tm, residual=x), h, s_final

    zero_state = jnp.zeros((bp, RWKV_HEADS, RWKV_HEAD, RWKV_HEAD), F32)
    xp, hp, wkv_p = rwkv_layer(xp, None, zero_state, bp, tp, 128)
    xs, hs, wkv_s = rwkv_layer(xs, state_tshift[0], state_wkv[0], bs, ts, ts)
    shift_p = hp.reshape(bp, tp, d)[:, -1]
    shift_s = hs.reshape(bs, ts, d)[:, -1]

    def ffn(x, i):
        w_router = jnp.pad(jnp.concatenate([moe_wgroup[i], moe_wexpert[i]], axis=1),
                           ((0, 0), (0, LANES - N_GROUPS - N_EXPERTS)))
        b_router = jnp.pad(jnp.concatenate([moe_bgroup[i], moe_bexpert[i]]),
                           (0, LANES - N_GROUPS - N_EXPERTS)).reshape(1, LANES)
        return _ffn(x, norm_ffn[i], w_router, b_router,
                    moe_wgate[i].astype(BF16), moe_wup[i].astype(BF16), moe_wdown[i].astype(BF16))

    xp = ffn(xp, 0)
    xs = ffn(xs, 0)

    w_qkv = jnp.stack(jnp.split(sb_wqkv[0], 3, axis=-1)).astype(BF16)
    w_att_out = sb_wo[0].astype(BF16)
    bias_lanes = jnp.broadcast_to(sb_bias[0][:, None, None], (SB_HEADS, 1, LANES)).astype(F32)
    bias_row = jnp.pad(sb_bias[0], (0, LANES - SB_HEADS)).reshape(1, LANES)
    no_mix = jnp.zeros((3, 1, d), F32)

    def qkv_proj(x):
        tm = _row_tile(x.shape[0])
        h = _rmsnorm(x, norm_mix[1], tm)
        return _rwkv_rkv(h, h, no_mix, w_qkv, tm, 1), tm

    qkv_p, tm_p = qkv_proj(xp)
    att_p = _sb_prompt(qkv_p, bias_lanes, bp, tp)
    xp = _matmul(att_p, w_att_out, tm_p, residual=xp)

    qkv_s, tm_s = qkv_proj(xs)
    n_phys = cache_k.shape[1]
    att_s = _sb_sample(qkv_s[0], qkv_s[1], qkv_s[2],
                       cache_k[0].reshape(n_phys, PAGE_SIZE, d), cache_v[0].reshape(n_phys, PAGE_SIZE, d),
                       page_table, bias_row)
    xs = _matmul(att_s, w_att_out, tm_s, residual=xs)

    xp = ffn(xp, 1)
    xs = ffn(xs, 1)

    y_prompt = _rmsnorm(xp, norm_final, _row_tile(xp.shape[0])).reshape(bp, tp, d)
    y_sample = _rmsnorm(xs, norm_final, _row_tile(xs.shape[0])).reshape(bs, ts, d)
    heads5 = lambda x, b, t: x.reshape(1, b, t, SB_HEADS, SB_HEAD_DIM)
    return (y_prompt, y_sample,
            shift_p[None], wkv_p[None],
            heads5(qkv_p[1], bp, tp), heads5(qkv_p[2], bp, tp),
            shift_s[None], wkv_s[None],
            heads5(qkv_s[1], bs, ts), heads5(qkv_s[2], bs, ts))
```

```python
import functools

import jax
import jax.numpy as jnp
from jax import lax
from jax.experimental import pallas as pl
from jax.experimental.pallas import tpu as pltpu

F32 = jnp.float32
BF16 = jnp.bfloat16

D_MODEL = 2048
RWKV_HEAD = 64
RWKV_HEADS = D_MODEL // RWKV_HEAD
HEAD_PAIRS = RWKV_HEADS // 2
LORA_PAD = 128
GN_EPS = 64e-5
RMS_EPS = 1e-6
SB_HEADS = 16
SB_HEAD_DIM = D_MODEL // SB_HEADS
Q_BLOCK = 128
PAGE_SIZE = 128
N_GROUPS = 4
EXPERTS_PER_GROUP = 8
N_EXPERTS = N_GROUPS * EXPERTS_PER_GROUP
D_EXPERT = 256
LANES = 128
SUBLANES = 8
VMEM_LIMIT_BYTES = 52 * 1024 * 1024


def _params(*semantics):
    return pltpu.CompilerParams(dimension_semantics=semantics, vmem_limit_bytes=VMEM_LIMIT_BYTES)


def _softplus(x):
    return jnp.maximum(x, 0.0) + jnp.log1p(jnp.exp(-jnp.abs(x)))


def _sigmoid(x):
    return 1.0 / (1.0 + jnp.exp(-x))


def _dot(a, b):
    return jnp.dot(a, b, preferred_element_type=F32)


def _dot_hilo(a, b_bf16):
    hi = a.astype(BF16)
    lo = (a - hi.astype(F32)).astype(BF16)
    return _dot(hi, b_bf16) + _dot(lo, b_bf16)


def _block_ones(n, seg):
    r = lax.broadcasted_iota(jnp.int32, (n, n), 0) // seg
    c = lax.broadcasted_iota(jnp.int32, (n, n), 1) // seg
    return jnp.where(r == c, 1.0, 0.0).astype(BF16)


def _rmsnorm_body(x_ref, g_ref, o_ref):
    x = x_ref[...]
    ms = jnp.mean(x * x, axis=-1, keepdims=True)
    o_ref[...] = x * lax.rsqrt(ms + RMS_EPS) * g_ref[...]


def _rmsnorm(x, g, tm):
    n, d = x.shape
    return pl.pallas_call(
        _rmsnorm_body,
        grid=(n // tm,),
        in_specs=[pl.BlockSpec((tm, d), lambda i: (i, 0)), pl.BlockSpec((1, d), lambda i: (0, 0))],
        out_specs=pl.BlockSpec((tm, d), lambda i: (i, 0)),
        out_shape=jax.ShapeDtypeStruct((n, d), F32),
        compiler_params=_params("parallel"),
        name="rmsnorm",
    )(x, g.reshape(1, d))


def _mm_body(x_ref, w_ref, o_ref):
    o_ref[...] = _dot(x_ref[...].astype(BF16), w_ref[...])


def _mm_res_body(x_ref, w_ref, r_ref, o_ref):
    o_ref[...] = r_ref[...] + _dot(x_ref[...].astype(BF16), w_ref[...])


def _matmul(x, w, tm, residual=None):
    n, k = x.shape
    m = w.shape[1]
    row = lambda i: (i, 0)
    in_specs = [pl.BlockSpec((tm, k), row), pl.BlockSpec((k, m), lambda i: (0, 0))]
    args = [x, w]
    body = _mm_body
    if residual is not None:
        in_specs.append(pl.BlockSpec((tm, m), row))
        args.append(residual)
        body = _mm_res_body
    return pl.pallas_call(
        body,
        grid=(n // tm,),
        in_specs=in_specs,
        out_specs=pl.BlockSpec((tm, m), row),
        out_shape=jax.ShapeDtypeStruct((n, m), F32),
        compiler_params=_params("parallel"),
        name="matmul",
    )(*args)


def _token_shift(h_ref, tail_ref, prev_ref, i, tm, seq_len):
    h = h_ref[...]
    if prev_ref is not None:
        return h, prev_ref[...]
    rolled = pltpu.roll(h, 1, 0)
    starts_sequence = (i * tm) % seq_len == 0
    first = jnp.where(starts_sequence, 0.0, tail_ref[SUBLANES - 1:SUBLANES, :])
    row = lax.broadcasted_iota(jnp.int32, h.shape, 0)
    return h, jnp.where(row == 0, first, rolled)


def _shift_specs(tm, d, has_prev, grid_rank):
    if grid_rank == 1:
        cur = lambda i: (i, 0)
        tail = lambda i: (jnp.maximum(i * (tm // SUBLANES) - 1, 0), 0)
    else:
        cur = lambda p, i: (i, 0)
        tail = lambda p, i: (jnp.maximum(i * (tm // SUBLANES) - 1, 0), 0)
    if has_prev:
        return [pl.BlockSpec((tm, d), cur), pl.BlockSpec((tm, d), cur)]
    return [pl.BlockSpec((tm, d), cur), pl.BlockSpec((SUBLANES, d), tail)]


def _rkv_body(h_ref, aux_ref, mu_ref, w_ref, o_ref, *, tm, seq_len, has_prev):
    i = pl.program_id(1)
    h, prev = _token_shift(h_ref, None if has_prev else aux_ref, aux_ref if has_prev else None, i, tm, seq_len)
    x = h + (prev - h) * mu_ref[0]
    o_ref[0] = _dot(x.astype(BF16), w_ref[0])


def _rwkv_rkv(h, prev, mu3, w3, tm, seq_len):
    n, d = h.shape
    has_prev = prev is not None
    return pl.pallas_call(
        functools.partial(_rkv_body, tm=tm, seq_len=seq_len, has_prev=has_prev),
        grid=(3, n // tm),
        in_specs=_shift_specs(tm, d, has_prev, 2) + [
            pl.BlockSpec((1, 1, d), lambda p, i: (p, 0, 0)),
            pl.BlockSpec((1, d, d), lambda p, i: (p, 0, 0)),
        ],
        out_specs=pl.BlockSpec((1, tm, d), lambda p, i: (p, i, 0)),
        out_shape=jax.ShapeDtypeStruct((3, n, d), F32),
        compiler_params=_params("arbitrary", "arbitrary"),
        name="rwkv_rkv",
    )(h, prev if has_prev else h, mu3, w3)


def _lora_body(h_ref, aux_ref, mu_ref, w0_ref, a0_ref, w1_ref, w2_ref, a1_ref, a2_ref, g1_ref, g2_ref,
               decay_ref, a_ref, g_ref, *, tm, seq_len, has_prev):
    i = pl.program_id(0)
    h, prev = _token_shift(h_ref, None if has_prev else aux_ref, aux_ref if has_prev else None, i, tm, seq_len)
    xx = prev - h
    xw = (h + xx * mu_ref[0]).astype(BF16)
    xa = (h + xx * mu_ref[1]).astype(BF16)
    xg = (h + xx * mu_ref[2]).astype(BF16)
    lw = _dot(jnp.tanh(_dot(xw, w1_ref[...])).astype(BF16), w2_ref[...])
    w = -_softplus(-(w0_ref[...] + lw)) - 0.5
    decay_ref[...] = -jnp.exp(w)
    la = _dot(_dot(xa, a1_ref[...]).astype(BF16), a2_ref[...])
    a_ref[...] = _sigmoid(a0_ref[...] + la)
    g_ref[...] = _dot(_sigmoid(_dot(xg, g1_ref[...])).astype(BF16), g2_ref[...])


def _rwkv_lora(h, prev, mu3, w0, a0, w1, w2, a1, a2, g1, g2, tm, seq_len):
    n, d = h.shape
    has_prev = prev is not None
    full = lambda arr: pl.BlockSpec(arr.shape, lambda i, nd=arr.ndim: (0,) * nd)
    row = pl.BlockSpec((tm, d), lambda i: (i, 0))
    out = jax.ShapeDtypeStruct((n, d), F32)
    return pl.pallas_call(
        functools.partial(_lora_body, tm=tm, seq_len=seq_len, has_prev=has_prev),
        grid=(n // tm,),
        in_specs=_shift_specs(tm, d, has_prev, 1) + [full(mu3), full(w0), full(a0), full(w1), full(w2),
                                                     full(a1), full(a2), full(g1), full(g2)],
        out_specs=[row, row, row],
        out_shape=[out, out, out],
        compiler_params=_params("parallel"),
        name="rwkv_lora",
    )(h, prev if has_prev else h, mu3, w0, a0, w1, w2, a1, a2, g1, g2)


PAIR_GROUP = 4


def _rows8(x):
    if x.shape[0] >= SUBLANES:
        return x
    return jnp.broadcast_to(x[0:1], (SUBLANES, x.shape[1]))


def _segsum(x, ones_bd):
    m = x.shape[0]
    return _dot_hilo(_rows8(x), ones_bd)[:m]


def _wkv_body(r_ref, k_ref, v_ref, d_ref, a_ref, g_ref, kkw_ref, kaw_ref, rkw_ref, lnw_ref, lnb_ref, s0_ref,
              o_ref, st_ref,
              s_scr, kk_scr, kp_scr, ka_scr, dr_scr, c1_scr, c2_scr, dd_scr, vv_scr, y_scr, *, tc):
    t_chunk = pl.program_id(1)
    unroll = min(tc, SUBLANES)
    ones_bd = _block_ones(LANES, RWKV_HEAD)
    rows = lax.broadcasted_iota(jnp.int32, (RWKV_HEAD, LANES), 0)
    lanes = lax.broadcasted_iota(jnp.int32, (RWKV_HEAD, LANES), 1)
    diag = (lanes % RWKV_HEAD) == rows

    @pl.when(t_chunk == 0)
    def _load_state():
        for p in range(HEAD_PAIRS):
            s_scr[p] = jnp.concatenate([s0_ref[0, 2 * p], s0_ref[0, 2 * p + 1]], axis=1)

    for p in range(HEAD_PAIRS):
        sl = slice(p * LANES, (p + 1) * LANES)
        r = r_ref[0, :, sl]
        k = k_ref[0, :, sl]
        a = a_ref[0, :, sl]
        kk = k * kkw_ref[:, sl]
        norm = jnp.sqrt(_segsum(kk * kk, ones_bd))
        kk = kk / jnp.maximum(norm, 1e-12)
        kp = k * (1.0 + (a - 1.0) * kaw_ref[:, sl])
        ka = kk * a
        d = jnp.exp(d_ref[0, :, sl])
        kk_scr[p] = _rows8(kk)
        kp_scr[p] = _rows8(kp)
        ka_scr[p] = _rows8(ka)
        dr_scr[p] = _rows8(d * r)
        c1_scr[p] = _rows8(_segsum(ka * r, ones_bd))
        c2_scr[p] = _rows8(_segsum(kp * r, ones_bd))
        dd_scr[p] = _rows8(d)
        vv_scr[p] = _rows8(v_ref[0, :, sl])

    blocks = tc // unroll

    def group_steps(idx, carry):
        grp = idx // blocks
        t0 = pl.multiple_of((idx % blocks) * unroll, unroll) if blocks > 1 else 0
        for pp in range(PAIR_GROUP):
            p = grp * PAIR_GROUP + pp
            rows_of = lambda ref: ref[p, pl.ds(t0, SUBLANES), :]
            kk8, dr8, ka8, kp8, c18, c28, dd8, vv8 = [rows_of(ref) for ref in (
                kk_scr, dr_scr, ka_scr, kp_scr, c1_scr, c2_scr, dd_scr, vv_scr)]
            s = s_scr[p]
            y_rows = []
            for j in range(unroll):
                row = lambda x: x[j:j + 1, :]
                s_kk = _dot_hilo(s * row(kk8), ones_bd)
                y_part = _dot_hilo(s * row(dr8), ones_bd)
                v_col = _dot_hilo(jnp.where(diag, row(vv8), 0.0), ones_bd)
                s = s * row(dd8) - s_kk * row(ka8) + v_col * row(kp8)
                y_col = y_part - s_kk * row(c18) + v_col * row(c28)
                y_rows.append(jnp.sum(jnp.where(diag, y_col, 0.0), axis=0, keepdims=True))
            s_scr[p] = s
            y_scr[p, pl.ds(t0, SUBLANES), :] = _rows8(jnp.concatenate(y_rows, axis=0))
        return carry
    lax.fori_loop(0, (HEAD_PAIRS // PAIR_GROUP) * blocks, group_steps, 0)

    for p in range(HEAD_PAIRS):
        sl = slice(p * LANES, (p + 1) * LANES)
        y = y_scr[p, 0:tc, :]
        mean = _segsum(y, ones_bd) * (1.0 / RWKV_HEAD)
        yc = y - mean
        var = _segsum(yc * yc, ones_bd) * (1.0 / RWKV_HEAD)
        yn = yc * lax.rsqrt(var + GN_EPS) * lnw_ref[:, sl] + lnb_ref[:, sl]
        bonus = _segsum(r_ref[0, :, sl] * kp_scr[p, 0:tc, :] * rkw_ref[:, sl], ones_bd) * v_ref[0, :, sl]
        o_ref[0, :, sl] = (yn + bonus) * g_ref[0, :, sl]

    @pl.when(t_chunk == pl.num_programs(1) - 1)
    def _store_state():
        for p in range(HEAD_PAIRS):
            s = s_scr[p]
            st_ref[0, 2 * p] = s[:, :RWKV_HEAD]
            st_ref[0, 2 * p + 1] = s[:, RWKV_HEAD:]


def _wkv(rkv, decay, a, g, kkw, kaw, rkw, lnw, lnb, s0, batch, seq_len, tc):
    d = D_MODEL
    rkv4 = rkv.reshape(3, batch, seq_len, d)
    seq3 = lambda x: x.reshape(batch, seq_len, d)
    tok = lambda which: pl.BlockSpec((None, 1, tc, d), lambda b, t, which=which: (which, b, t, 0))
    tok3 = pl.BlockSpec((1, tc, d), lambda b, t: (b, t, 0))
    vec = pl.BlockSpec((1, d), lambda b, t: (0, 0))
    state = pl.BlockSpec((1, RWKV_HEADS, RWKV_HEAD, RWKV_HEAD), lambda b, t: (b, 0, 0, 0))
    chunk = pltpu.VMEM((HEAD_PAIRS, max(tc, SUBLANES), LANES), F32)
    o, st = pl.pallas_call(
        functools.partial(_wkv_body, tc=tc),
        grid=(batch, seq_len // tc),
        in_specs=[tok(0), tok(1), tok(2), tok3, tok3, tok3, vec, vec, vec, vec, vec, state],
        out_specs=[tok3, state],
        out_shape=[jax.ShapeDtypeStruct((batch, seq_len, d), F32),
                   jax.ShapeDtypeStruct((batch, RWKV_HEADS, RWKV_HEAD, RWKV_HEAD), F32)],
        scratch_shapes=[pltpu.VMEM((HEAD_PAIRS, RWKV_HEAD, LANES), F32)] + [chunk] * 9,
        compiler_params=_params("parallel", "arbitrary"),
        name="wkv",
    )(rkv4, rkv4, rkv4, seq3(decay), seq3(a), seq3(g), kkw, kaw, rkw, lnw, lnb, s0)
    return o.reshape(batch * seq_len, d), st


WKV_CHUNK = 64
WKV_STEP_PAIRS = 8
WKV_STEP_LANES = WKV_STEP_PAIRS * LANES


def _dot_nt(a, b):
    return lax.dot_general(a, b, (((1,), (1,)), ((), ())), preferred_element_type=F32)


def _split3(x):
    hi = x.astype(BF16)
    r1 = x - hi.astype(F32)
    mid = r1.astype(BF16)
    lo = (r1 - mid.astype(F32)).astype(BF16)
    return hi, mid, lo


def _wkv_chunk_body(r_ref, k_ref, v_ref, ld_ref, a_ref, g_ref, kkw_ref, kaw_ref, rkw_ref, lnw_ref, lnb_ref,
                    o_ref, st_ref, s_scr):
    t_chunk = pl.program_id(2)
    c = WKV_CHUNK
    ones_bd = _block_ones(LANES, RWKV_HEAD)
    row = lax.broadcasted_iota(jnp.int32, (LANES, LANES), 0)
    col = lax.broadcasted_iota(jnp.int32, (LANES, LANES), 1)
    strict = col < row
    incl = col <= row
    eye = jnp.where(row == col, 1.0, 0.0)
    ltri = jnp.where(incl, 1.0, 0.0).astype(BF16)
    head0 = lax.broadcasted_iota(jnp.int32, (c, LANES), 1) < RWKV_HEAD
    stack = lambda x: jnp.concatenate([jnp.where(head0, x, 0.0), jnp.where(head0, 0.0, x)], axis=0)

    @pl.when(t_chunk == 0)
    def _zero_state():
        s_scr[...] = jnp.zeros_like(s_scr)

    pairs = range(WKV_STEP_PAIRS)
    lanes_of = [slice(pp * LANES, (pp + 1) * LANES) for pp in pairs]
    each = lambda fn, *lists: [fn(*args) for args in zip(*lists)]
    bf = lambda x: x.astype(BF16)
    r = [r_ref[0, :, sl] for sl in lanes_of]
    k = [k_ref[0, :, sl] for sl in lanes_of]
    v = [v_ref[0, :, sl] for sl in lanes_of]
    ld = [ld_ref[0, :, sl] for sl in lanes_of]
    a = [a_ref[0, :, sl] for sl in lanes_of]
    kk_raw = each(lambda x, sl: x * kkw_ref[:, sl], k, lanes_of)
    kk_norm = each(lambda x: jnp.sqrt(_segsum(x * x, ones_bd)), kk_raw)
    kk = each(lambda x, n: x / jnp.maximum(n, 1e-12), kk_raw, kk_norm)
    kp = each(lambda x, y, sl: x * (1.0 + (y - 1.0) * kaw_ref[:, sl]), k, a, lanes_of)
    ka = each(lambda x, y: x * y, kk, a)
    ld_pieces = each(lambda x: _split3(jnp.concatenate([x, jnp.zeros_like(x)], axis=0)), ld)
    cum2 = [_dot(ltri, p3[0]) + _dot(ltri, p3[1]) + _dot(ltri, p3[2]) for p3 in ld_pieces]
    cum = [x[:c] for x in cum2]
    gam_end = [jnp.exp(x[c:]) for x in cum2]
    gam_inv = [jnp.exp(-x) for x in cum]
    at = each(lambda x, y: x * y, ka, gam_inv)
    kt = each(lambda x, y: x * y, kp, gam_inv)
    x_st = each(lambda kk_, r_, cum_, ld_: bf(jnp.concatenate(
        [stack(kk_ * jnp.exp(cum_ - ld_)), stack(r_ * jnp.exp(cum_))], axis=0)), kk, r, cum, ld)
    y_st = each(lambda x, y: bf(jnp.concatenate([stack(x), stack(y)], axis=0)), at, kt)
    gram = each(_dot_nt, x_st, y_st)
    m_a = [jnp.where(strict, x[:LANES, :LANES], 0.0) for x in gram]
    m_k = [bf(jnp.where(strict, x[:LANES, LANES:], 0.0)) for x in gram]
    n_a = [bf(jnp.where(incl, x[LANES:, :LANES], 0.0)) for x in gram]
    n_k = [bf(jnp.where(incl, x[LANES:, LANES:], 0.0)) for x in gram]
    power = [-x for x in m_a]
    inv = [eye + x for x in power]
    for _ in range(5):
        power = [_dot(bf(x), bf(x)) for x in power]
        inv = each(lambda t, p: t + _dot(bf(t), bf(p)), inv, power)
    inv = [bf(x) for x in inv]
    v_st = [stack(x) for x in v]
    v_stb = [bf(x) for x in v_st]
    v_stt = [bf(x.T) for x in v_st]
    at_end = each(lambda x, y: bf(stack(x * y)), at, gam_end)
    kt_end = each(lambda x, y: bf(stack(x * y)), kt, gam_end)
    s_from_v = each(_dot, v_stt, kt_end)
    m_kv = each(_dot, m_k, v_stb)
    n_kv = each(_dot, n_k, v_stb)
    s = [s_scr[pp] for pp in pairs]
    proj = each(lambda x, s_: _dot_nt(x, bf(s_)), x_st, s)
    u = each(lambda t, p, mv: -_dot(t, bf(p[:LANES] + mv)), inv, proj, m_kv)
    y2 = each(lambda p, n, u_, nv: p[LANES:] + _dot(n, bf(u_)) + nv, proj, n_a, u, n_kv)
    s_from_u = each(lambda u_, x: _dot(bf(u_.T), x), u, at_end)
    for pp in pairs:
        s_scr[pp] = s[pp] * jnp.concatenate([gam_end[pp], gam_end[pp]], axis=0) + s_from_u[pp] + s_from_v[pp]
    y = [x[:c] + x[c:] for x in y2]
    mean = [_segsum(x, ones_bd) * (1.0 / RWKV_HEAD) for x in y]
    yc = each(lambda x, m: x - m, y, mean)
    var = [_segsum(x * x, ones_bd) * (1.0 / RWKV_HEAD) for x in yc]
    bonus_dot = each(lambda r_, kp_, sl: _segsum(r_ * kp_ * rkw_ref[:, sl], ones_bd), r, kp, lanes_of)
    for pp in pairs:
        sl = lanes_of[pp]
        yn = yc[pp] * lax.rsqrt(var[pp] + GN_EPS) * lnw_ref[:, sl] + lnb_ref[:, sl]
        o_ref[0, :, sl] = (yn + bonus_dot[pp] * v[pp]) * g_ref[0, :, sl]

    @pl.when(t_chunk == pl.num_programs(2) - 1)
    def _store_state():
        for pp in range(WKV_STEP_PAIRS):
            s = s_scr[pp]
            st_ref[0, 2 * pp] = s[:RWKV_HEAD, :RWKV_HEAD]
            st_ref[0, 2 * pp + 1] = s[RWKV_HEAD:, RWKV_HEAD:]


def _wkv_chunked(rkv, logd, a, g, kkw, kaw, rkw, lnw, lnb, batch, seq_len):
    d = D_MODEL
    c = WKV_CHUNK
    w = WKV_STEP_LANES
    rkv4 = rkv.reshape(3, batch, seq_len, d)
    seq3 = lambda x: x.reshape(batch, seq_len, d)
    tok = lambda which: pl.BlockSpec((None, 1, c, w), lambda b, p, t, which=which: (which, b, t, p))
    tok3 = pl.BlockSpec((1, c, w), lambda b, p, t: (b, t, p))
    vec = pl.BlockSpec((1, w), lambda b, p, t: (0, p))
    state = pl.BlockSpec((1, 2 * WKV_STEP_PAIRS, RWKV_HEAD, RWKV_HEAD), lambda b, p, t: (b, p, 0, 0))
    o, st = pl.pallas_call(
        _wkv_chunk_body,
        grid=(batch, d // w, seq_len // c),
        in_specs=[tok(0), tok(1), tok(2), tok3, tok3, tok3, vec, vec, vec, vec, vec],
        out_specs=[tok3, state],
        out_shape=[jax.ShapeDtypeStruct((batch, seq_len, d), F32),
                   jax.ShapeDtypeStruct((batch, RWKV_HEADS, RWKV_HEAD, RWKV_HEAD), F32)],
        scratch_shapes=[pltpu.VMEM((WKV_STEP_PAIRS, LANES, LANES), F32)],
        compiler_params=_params("parallel", "parallel", "arbitrary"),
        name="wkv_chunked",
    )(rkv4, rkv4, rkv4, seq3(logd), seq3(a), seq3(g), kkw, kaw, rkw, lnw, lnb)
    return o.reshape(batch * seq_len, d), st


SB_STEP_HEADS = 8
SB_STEP_LANES = SB_STEP_HEADS * SB_HEAD_DIM


def _sb_prompt_body(q_ref, k_ref, v_ref, b_ref, o_ref, acc_scr, tail_scr):
    qi = pl.program_id(2)
    scale = SB_HEAD_DIM ** -0.5
    t_idx = lax.broadcasted_iota(jnp.int32, (Q_BLOCK, Q_BLOCK), 0)
    s_idx = lax.broadcasted_iota(jnp.int32, (Q_BLOCK, Q_BLOCK), 1)
    later = jnp.where(t_idx > s_idx, 1.0, 0.0).astype(BF16)
    later_and_ones = jnp.concatenate([later, jnp.ones((Q_BLOCK, Q_BLOCK), BF16)], axis=1)

    acc_scr[...] = jnp.zeros_like(acc_scr)
    tail_scr[...] = jnp.zeros_like(tail_scr)

    def body(j, carry):
        kb = qi - j
        start = pl.multiple_of(kb * Q_BLOCK, Q_BLOCK)
        mask = (kb * Q_BLOCK + s_idx) < (qi * Q_BLOCK + t_idx)
        heads = range(SB_STEP_HEADS)
        lanes_of = [slice(hh * SB_HEAD_DIM, (hh + 1) * SB_HEAD_DIM) for hh in heads]
        z = [_dot_nt(q_ref[0, :, sl].astype(BF16), k_ref[0, pl.ds(start, Q_BLOCK), sl].astype(BF16)) * scale
             + b_ref[:, sl] for sl in lanes_of]
        sp = [_softplus(x) for x in z]
        log_not = [jnp.where(mask, -x, 0.0) for x in sp]
        hi = [x.astype(BF16) for x in log_not]
        lo = [(x - h.astype(F32)).astype(BF16) for x, h in zip(log_not, hi)]
        sums_hi = [_dot(x, later_and_ones) for x in hi]
        sums_lo = [_dot(x, later_and_ones) for x in lo]
        sums = [x + y for x, y in zip(sums_hi, sums_lo)]
        tail = [tail_scr[hh] for hh in heads]
        p = [jnp.where(mask, jnp.exp(z[hh] - sp[hh] + sums[hh][:, :Q_BLOCK] + tail[hh]), 0.0).astype(BF16)
             for hh in heads]
        pv = [_dot(p[hh], v_ref[0, pl.ds(start, Q_BLOCK), lanes_of[hh]].astype(BF16)) for hh in heads]
        for hh in heads:
            acc_scr[hh] += pv[hh]
            tail_scr[hh] = tail[hh] + sums[hh][:, Q_BLOCK:]
        return carry

    lax.fori_loop(0, qi + 1, body, 0)
    for hh in range(SB_STEP_HEADS):
        o_ref[0, :, hh * SB_HEAD_DIM:(hh + 1) * SB_HEAD_DIM] = acc_scr[hh]


def _sb_prompt(qkv, bias_cols, batch, seq_len):
    d = D_MODEL
    w = SB_STEP_LANES
    qkv4 = qkv.reshape(3, batch, seq_len, d)
    qspec = pl.BlockSpec((None, 1, Q_BLOCK, w), lambda b, h, i: (0, b, i, h))
    kspec = pl.BlockSpec((None, 1, seq_len, w), lambda b, h, i: (1, b, 0, h))
    vspec = pl.BlockSpec((None, 1, seq_len, w), lambda b, h, i: (2, b, 0, h))
    block_state = pltpu.VMEM((SB_STEP_HEADS, Q_BLOCK, SB_HEAD_DIM), F32)
    out = pl.pallas_call(
        _sb_prompt_body,
        grid=(batch, d // w, seq_len // Q_BLOCK),
        in_specs=[qspec, kspec, vspec, pl.BlockSpec((1, w), lambda b, h, i: (0, h))],
        out_specs=pl.BlockSpec((1, Q_BLOCK, w), lambda b, h, i: (b, i, h)),
        out_shape=jax.ShapeDtypeStruct((batch, seq_len, d), F32),
        scratch_shapes=[block_state, block_state],
        compiler_params=_params("parallel", "parallel", "arbitrary"),
        name="sb_prompt",
    )(qkv4, qkv4, qkv4, bias_cols)
    return out.reshape(batch * seq_len, d)


PAGE_ROWS = PAGE_SIZE * SB_HEADS


SB_STEP_SEQS = 2


def _sb_sample_body(pt_ref, q_ref, kn_ref, vn_ref, *refs):
    n = SB_STEP_SEQS
    kc_refs, vc_refs = refs[:n], refs[n:2 * n]
    b_ref, col_ref, exp_ref, o_ref, acc_scr, tail_scr = refs[2 * n:]
    seqs = range(n)
    step = pl.program_id(1)
    n_steps = pl.num_programs(1)
    scale = SB_HEAD_DIM ** -0.5
    bias = b_ref[...]
    head = lax.broadcasted_iota(jnp.int32, (SB_HEADS, PAGE_ROWS), 0)
    own_head = lax.broadcasted_iota(jnp.int32, (SB_HEADS, PAGE_ROWS), 1) % SB_HEADS == head
    r_idx = lax.broadcasted_iota(jnp.int32, (PAGE_SIZE, PAGE_SIZE), 0)
    c_idx = lax.broadcasted_iota(jnp.int32, (PAGE_SIZE, PAGE_SIZE), 1)
    later = jnp.where(r_idx > c_idx, 1.0, 0.0).astype(BF16)
    ones = jnp.ones((PAGE_SIZE, PAGE_SIZE), BF16)

    @pl.when(step == 0)
    def _init():
        n_past = n_steps * PAGE_SIZE
        new_is_before = jnp.full((SB_HEADS, LANES), n_past, jnp.int32) < n_past
        for i in seqs:
            z_new = jnp.sum(q_ref[i] * kn_ref[i], axis=-1, keepdims=True) * scale + bias
            sp_new = _softplus(z_new)
            tail_scr[i] = jnp.where(new_is_before, -sp_new, 0.0)
            acc_scr[i] = jnp.where(new_is_before, jnp.exp(z_new - sp_new), 0.0) * vn_ref[i]

    z_all = [_dot_nt(q_ref[i].astype(BF16), kc_refs[i][0].astype(BF16)) for i in seqs]
    z_own = [_split3(jnp.where(own_head, x, 0.0)) for x in z_all]
    collapse = col_ref[...]
    z = [(_dot(p3[0], collapse) + _dot(p3[1], collapse) + _dot(p3[2], collapse)) * scale + bias for p3 in z_own]
    sp = [_softplus(x) for x in z]
    hi = [(-x).astype(BF16) for x in sp]
    lo = [(-x - h.astype(F32)).astype(BF16) for x, h in zip(sp, hi)]
    after = [_dot(h, later) + _dot(l, later) for h, l in zip(hi, lo)]
    total = [_dot(h, ones) + _dot(l, ones) for h, l in zip(hi, lo)]
    tail = [tail_scr[i] for i in seqs]
    p = [jnp.exp(z[i] - sp[i] + after[i] + tail[i]).astype(BF16) for i in seqs]
    p_rows = [jnp.where(own_head, _dot(x, exp_ref[...]), 0.0).astype(BF16) for x in p]
    pv = [_dot(p_rows[i], vc_refs[i][0].astype(BF16)) for i in seqs]
    for i in seqs:
        acc_scr[i] += pv[i]
        tail_scr[i] = tail[i] + total[i]

    @pl.when(step == n_steps - 1)
    def _finish():
        o_ref[...] = acc_scr[...]


def _sb_sample(q, k_new, v_new, cache_k, cache_v, page_table, bias):
    batch, d = q.shape
    n_pages = page_table.shape[1]
    n_phys = cache_k.shape[0]
    pages = lambda c: c.reshape(n_phys, PAGE_ROWS, SB_HEAD_DIM)
    row_token = lax.broadcasted_iota(jnp.int32, (PAGE_ROWS, PAGE_SIZE), 0) // SB_HEADS
    collapse = (row_token == lax.broadcasted_iota(jnp.int32, (PAGE_ROWS, PAGE_SIZE), 1)).astype(BF16)
    bias_lanes = jnp.broadcast_to(bias[:, None], (SB_HEADS, LANES)).astype(F32)
    n = SB_STEP_SEQS
    tok = pl.BlockSpec((n, SB_HEADS, SB_HEAD_DIM), lambda b, s, pt: (b, 0, 0))
    page = lambda i: pl.BlockSpec((1, PAGE_ROWS, SB_HEAD_DIM),
                                  lambda b, s, pt, i=i: (pt[b * n + i, n_pages - 1 - s], 0, 0))
    const = lambda shape: pl.BlockSpec(shape, lambda b, s, pt: (0, 0))
    heads3 = lambda x: x.reshape(batch, SB_HEADS, SB_HEAD_DIM)
    per_seq = pltpu.VMEM((n, SB_HEADS, SB_HEAD_DIM), F32)
    out = pl.pallas_call(
        _sb_sample_body,
        grid_spec=pltpu.PrefetchScalarGridSpec(
            num_scalar_prefetch=1,
            grid=(batch // n, n_pages),
            in_specs=[tok, tok, tok] + [page(i) for i in range(n)] * 2 + [
                const((SB_HEADS, LANES)), const((PAGE_ROWS, PAGE_SIZE)), const((PAGE_SIZE, PAGE_ROWS))],
            out_specs=tok,
            scratch_shapes=[per_seq, per_seq],
        ),
        out_shape=jax.ShapeDtypeStruct((batch, SB_HEADS, SB_HEAD_DIM), F32),
        compiler_params=_params("parallel", "arbitrary"),
        name="sb_sample",
    )(page_table, heads3(q), heads3(k_new), heads3(v_new), *([pages(cache_k)] * n), *([pages(cache_v)] * n),
      bias_lanes, collapse, collapse.T)
    return out.reshape(batch, d)


ROUTER_OFFSET = N_GROUPS


def _router_body(x_ref, w_ref, b_ref, gate_ref):
    logits = jnp.dot(x_ref[...], w_ref[...], preferred_element_type=F32,
                     precision=lax.Precision.HIGHEST) + b_ref[...]
    lane = lax.broadcasted_iota(jnp.int32, logits.shape, 1)
    neg = -jnp.inf
    first_lane_of = lambda hit: jnp.min(jnp.where(hit, lane, LANES), axis=-1, keepdims=True)
    is_group = lane < N_GROUPS
    glog = jnp.where(is_group, logits, neg)
    gmax = jnp.max(glog, axis=-1, keepdims=True)
    g_sel = first_lane_of(glog == gmax)
    p_group = 1.0 / jnp.sum(jnp.exp(glog - gmax), axis=-1, keepdims=True)
    in_group = ((lane - ROUTER_OFFSET) // EXPERTS_PER_GROUP == g_sel) & (lane >= ROUTER_OFFSET) \
        & (lane < ROUTER_OFFSET + N_EXPERTS)
    elog = jnp.where(in_group, logits, neg)
    top1 = jnp.max(elog, axis=-1, keepdims=True)
    lane1 = first_lane_of(elog == top1)
    rest = jnp.where(lane == lane1, neg, elog)
    top2 = jnp.max(rest, axis=-1, keepdims=True)
    lane2 = first_lane_of(rest == top2)
    e2 = jnp.exp(top2 - top1)
    w1 = p_group / (1.0 + e2)
    w2 = p_group * e2 / (1.0 + e2)
    gate_ref[...] = jnp.where(lane == lane1, w1, 0.0) + jnp.where(lane == lane2, w2, 0.0)


def _router(x, w_router, b_router, tm):
    n, d = x.shape
    return pl.pallas_call(
        _router_body,
        grid=(n // tm,),
        in_specs=[pl.BlockSpec((tm, d), lambda i: (i, 0)),
                  pl.BlockSpec((d, LANES), lambda i: (0, 0)),
                  pl.BlockSpec((1, LANES), lambda i: (0, 0))],
        out_specs=pl.BlockSpec((tm, LANES), lambda i: (i, 0)),
        out_shape=jax.ShapeDtypeStruct((n, LANES), F32),
        compiler_params=_params("parallel"),
        name="moe_router",
    )(x, w_router, b_router)


def _moe_body(x_ref, gate_ref, res_ref, wg_ref, wu_ref, wd_ref, o_ref, xb_scr):
    e = pl.program_id(1)

    @pl.when(e == 0)
    def _init():
        xb_scr[...] = x_ref[...].astype(BF16)
        o_ref[...] = res_ref[...]

    xb = xb_scr[...]
    hg = _dot(xb, wg_ref[0])
    hu = _dot(xb, wu_ref[0])
    gate = gate_ref[...]
    lane = lax.broadcasted_iota(jnp.int32, gate.shape, 1)
    ge = jnp.sum(jnp.where(lane == e + ROUTER_OFFSET, gate, 0.0), axis=-1, keepdims=True)
    act = (hg * _sigmoid(hg)) * hu * ge
    o_ref[...] += _dot(act.astype(BF16), wd_ref[0])


def _moe_dense(x, gate, residual, wg, wu, wd, tm):
    n, d = x.shape
    row = lambda i, e: (i, 0)
    return pl.pallas_call(
        _moe_body,
        grid=(n // tm, N_EXPERTS),
        in_specs=[pl.BlockSpec((tm, d), row), pl.BlockSpec((tm, LANES), row), pl.BlockSpec((tm, d), row),
                  pl.BlockSpec((1, d, D_EXPERT), lambda i, e: (e, 0, 0)),
                  pl.BlockSpec((1, d, D_EXPERT), lambda i, e: (e, 0, 0)),
                  pl.BlockSpec((1, D_EXPERT, d), lambda i, e: (e, 0, 0))],
        out_specs=pl.BlockSpec((tm, d), row),
        out_shape=jax.ShapeDtypeStruct((n, d), F32),
        scratch_shapes=[pltpu.VMEM((tm, d), BF16)],
        compiler_params=_params("parallel", "arbitrary"),
        name="moe_experts",
    )(x, gate, residual, wg, wu, wd)


def _row_tile(n):
    return 256 if n % 256 == 0 else LANES


def _ffn(x, norm_g, w_router, b_router, wg, wu, wd):
    tm = _row_tile(x.shape[0])
    h = _rmsnorm(x, norm_g, tm)
    gate = _router(h, w_router, b_router, tm)
    return _moe_dense(h, gate, x, wg, wu, wd, 512 if x.shape[0] % 512 == 0 else LANES)


def kernel(x_prompt, x_sample, state_tshift, state_wkv, cache_k, cache_v, page_table, norm_mix, norm_ffn, norm_final, rwkv_mu, rwkv_w0, rwkv_w1, rwkv_w2, rwkv_a0, rwkv_a1, rwkv_a2, rwkv_g1, rwkv_g2, rwkv_kk, rwkv_ka, rwkv_rk, rwkv_wr, rwkv_wk, rwkv_wv, rwkv_wo, rwkv_lnw, rwkv_lnb, sb_wqkv, sb_wo, sb_bias, moe_wgroup, moe_bgroup, moe_wexpert, moe_bexpert, moe_wgate, moe_wup, moe_wdown):
    d = D_MODEL
    bp, tp, _ = x_prompt.shape
    bs, ts, _ = x_sample.shape
    xp = x_prompt.reshape(bp * tp, d)
    xs = x_sample.reshape(bs * ts, d)
    row1 = lambda v: v.reshape(1, d)
    pad_cols = lambda w: jnp.pad(w, ((0, 0), (0, LORA_PAD - w.shape[1]))).astype(BF16)
    pad_rows = lambda w: jnp.pad(w, ((0, LORA_PAD - w.shape[0]), (0, 0))).astype(BF16)

    mu = rwkv_mu[0]
    mu_rkv = jnp.stack([mu[0], mu[2], mu[3]]).reshape(3, 1, d)
    mu_lora = jnp.stack([mu[1], mu[4], mu[5]]).reshape(3, 1, d)
    w_rkv = jnp.stack([rwkv_wr[0], rwkv_wk[0], rwkv_wv[0]]).astype(BF16)
    lora = (row1(rwkv_w0[0]), row1(rwkv_a0[0]), pad_cols(rwkv_w1[0]), pad_rows(rwkv_w2[0]),
            pad_cols(rwkv_a1[0]), pad_rows(rwkv_a2[0]), rwkv_g1[0].astype(BF16), rwkv_g2[0].astype(BF16))
    head_vecs = (row1(rwkv_kk[0]), row1(rwkv_ka[0]), row1(rwkv_rk[0]), row1(rwkv_lnw[0]), row1(rwkv_lnb[0]))
    wo = rwkv_wo[0].astype(BF16)

    def rwkv_layer(x, prev, s0, batch, seq_len):
        tm = _row_tile(x.shape[0])
        h = _rmsnorm(x, norm_mix[0], tm)
        rkv = _rwkv_rkv(h, prev, mu_rkv, w_rkv, tm, seq_len)
        logd, a, g = _rwkv_lora(h, prev, mu_lora, *lora, tm, seq_len)
        if s0 is None:
            o, s_final = _wkv_chunked(rkv, logd, a, g, *head_vecs, batch, seq_len)
        else:
            o, s_final = _wkv(rkv, logd, a, g, *head_vecs, s0, batch, seq_len, seq_len)
        return _matmul(o, wo, tm, residual=x), h, s_final

    xp, hp, wkv_p = rwkv_layer(xp, None, None, bp, tp)
    xs, hs, wkv_s = rwkv_layer(xs, state_tshift[0], state_wkv[0], bs, ts)
    shift_p = hp.reshape(bp, tp, d)[:, -1]
    shift_s = hs.reshape(bs, ts, d)[:, -1]

    def ffn(x, i):
        w_router = jnp.pad(jnp.concatenate([moe_wgroup[i], moe_wexpert[i]], axis=1),
                           ((0, 0), (0, LANES - N_GROUPS - N_EXPERTS)))
        b_router = jnp.pad(jnp.concatenate([moe_bgroup[i], moe_bexpert[i]]),
                           (0, LANES - N_GROUPS - N_EXPERTS)).reshape(1, LANES)
        return _ffn(x, norm_ffn[i], w_router, b_router,
                    moe_wgate[i].astype(BF16), moe_wup[i].astype(BF16), moe_wdown[i].astype(BF16))

    xp = ffn(xp, 0)
    xs = ffn(xs, 0)

    w_qkv = jnp.stack(jnp.split(sb_wqkv[0], 3, axis=-1)).astype(BF16)
    w_att_out = sb_wo[0].astype(BF16)
    bias_cols = jnp.repeat(sb_bias[0].astype(F32), SB_HEAD_DIM).reshape(1, d)
    no_mix = jnp.zeros((3, 1, d), F32)

    def qkv_proj(x):
        tm = _row_tile(x.shape[0])
        h = _rmsnorm(x, norm_mix[1], tm)
        return _rwkv_rkv(h, h, no_mix, w_qkv, tm, 1), tm

    qkv_p, tm_p = qkv_proj(xp)
    att_p = _sb_prompt(qkv_p, bias_cols, bp, tp)
    xp = _matmul(att_p, w_att_out, tm_p, residual=xp)

    qkv_s, tm_s = qkv_proj(xs)
    att_s = _sb_sample(qkv_s[0], qkv_s[1], qkv_s[2], cache_k[0], cache_v[0], page_table, sb_bias[0])
    xs = _matmul(att_s, w_att_out, tm_s, residual=xs)

    xp = ffn(xp, 1)
    xs = ffn(xs, 1)

    y_prompt = _rmsnorm(xp, norm_final, _row_tile(xp.shape[0])).reshape(bp, tp, d)
    y_sample = _rmsnorm(xs, norm_final, _row_tile(xs.shape[0])).reshape(bs, ts, d)
    heads5 = lambda x, b, t: x.reshape(1, b, t, SB_HEADS, SB_HEAD_DIM)
    return (y_prompt, y_sample,
            shift_p[None], wkv_p[None],
            heads5(qkv_p[1], bp, tp), heads5(qkv_p[2], bp, tp),
            shift_s[None], wkv_s[None],
            heads5(qkv_s[1], bs, ts), heads5(qkv_s[2], bs, ts))
```

```python
import functools

import jax
import jax.numpy as jnp
from jax import lax
from jax.experimental import pallas as pl
from jax.experimental.pallas import tpu as pltpu

F32 = jnp.float32
BF16 = jnp.bfloat16

D_MODEL = 2048
RWKV_HEAD = 64
RWKV_HEADS = D_MODEL // RWKV_HEAD
HEAD_PAIRS = RWKV_HEADS // 2
LORA_PAD = 128
GN_EPS = 64e-5
RMS_EPS = 1e-6
SB_HEADS = 16
SB_HEAD_DIM = D_MODEL // SB_HEADS
Q_BLOCK = 128
PAGE_SIZE = 128
N_GROUPS = 4
EXPERTS_PER_GROUP = 8
N_EXPERTS = N_GROUPS * EXPERTS_PER_GROUP
D_EXPERT = 256
LANES = 128
SUBLANES = 8
VMEM_LIMIT_BYTES = 52 * 1024 * 1024


def _params(*semantics):
    return pltpu.CompilerParams(dimension_semantics=semantics, vmem_limit_bytes=VMEM_LIMIT_BYTES)


def _softplus(x):
    return jnp.maximum(x, 0.0) + jnp.log1p(jnp.exp(-jnp.abs(x)))


def _sigmoid(x):
    return 1.0 / (1.0 + jnp.exp(-x))


def _dot(a, b):
    return jnp.dot(a, b, preferred_element_type=F32)


def _dot_hilo(a, b_bf16):
    hi = a.astype(BF16)
    lo = (a - hi.astype(F32)).astype(BF16)
    return _dot(hi, b_bf16) + _dot(lo, b_bf16)


def _block_ones(n, seg):
    r = lax.broadcasted_iota(jnp.int32, (n, n), 0) // seg
    c = lax.broadcasted_iota(jnp.int32, (n, n), 1) // seg
    return jnp.where(r == c, 1.0, 0.0).astype(BF16)


def _rmsnorm_body(x_ref, g_ref, o_ref):
    x = x_ref[...]
    ms = jnp.mean(x * x, axis=-1, keepdims=True)
    o_ref[...] = x * lax.rsqrt(ms + RMS_EPS) * g_ref[...]


def _rmsnorm(x, g, tm):
    n, d = x.shape
    return pl.pallas_call(
        _rmsnorm_body,
        grid=(n // tm,),
        in_specs=[pl.BlockSpec((tm, d), lambda i: (i, 0)), pl.BlockSpec((1, d), lambda i: (0, 0))],
        out_specs=pl.BlockSpec((tm, d), lambda i: (i, 0)),
        out_shape=jax.ShapeDtypeStruct((n, d), F32),
        compiler_params=_params("parallel"),
        name="rmsnorm",
    )(x, g.reshape(1, d))


def _mm_body(x_ref, w_ref, o_ref):
    o_ref[...] = _dot(x_ref[...].astype(BF16), w_ref[...])


def _mm_res_body(x_ref, w_ref, r_ref, o_ref):
    o_ref[...] = r_ref[...] + _dot(x_ref[...].astype(BF16), w_ref[...])


def _matmul(x, w, tm, residual=None):
    n, k = x.shape
    m = w.shape[1]
    row = lambda i: (i, 0)
    in_specs = [pl.BlockSpec((tm, k), row), pl.BlockSpec((k, m), lambda i: (0, 0))]
    args = [x, w]
    body = _mm_body
    if residual is not None:
        in_specs.append(pl.BlockSpec((tm, m), row))
        args.append(residual)
        body = _mm_res_body
    return pl.pallas_call(
        body,
        grid=(n // tm,),
        in_specs=in_specs,
        out_specs=pl.BlockSpec((tm, m), row),
        out_shape=jax.ShapeDtypeStruct((n, m), F32),
        compiler_params=_params("parallel"),
        name="matmul",
    )(*args)


def _token_shift(h_ref, tail_ref, prev_ref, i, tm, seq_len):
    h = h_ref[...]
    if prev_ref is not None:
        return h, prev_ref[...]
    rolled = pltpu.roll(h, 1, 0)
    starts_sequence = (i * tm) % seq_len == 0
    first = jnp.where(starts_sequence, 0.0, tail_ref[SUBLANES - 1:SUBLANES, :])
    row = lax.broadcasted_iota(jnp.int32, h.shape, 0)
    return h, jnp.where(row == 0, first, rolled)


def _shift_specs(tm, d, has_prev, grid_rank):
    if grid_rank == 1:
        cur = lambda i: (i, 0)
        tail = lambda i: (jnp.maximum(i * (tm // SUBLANES) - 1, 0), 0)
    else:
        cur = lambda p, i: (i, 0)
        tail = lambda p, i: (jnp.maximum(i * (tm // SUBLANES) - 1, 0), 0)
    if has_prev:
        return [pl.BlockSpec((tm, d), cur), pl.BlockSpec((tm, d), cur)]
    return [pl.BlockSpec((tm, d), cur), pl.BlockSpec((SUBLANES, d), tail)]


def _rkv_body(h_ref, aux_ref, mu_ref, w_ref, o_ref, *, tm, seq_len, has_prev):
    i = pl.program_id(1)
    h, prev = _token_shift(h_ref, None if has_prev else aux_ref, aux_ref if has_prev else None, i, tm, seq_len)
    x = h + (prev - h) * mu_ref[0]
    o_ref[0] = _dot(x.astype(BF16), w_ref[0])


def _rwkv_rkv(h, prev, mu3, w3, tm, seq_len):
    n, d = h.shape
    has_prev = prev is not None
    return pl.pallas_call(
        functools.partial(_rkv_body, tm=tm, seq_len=seq_len, has_prev=has_prev),
        grid=(3, n // tm),
        in_specs=_shift_specs(tm, d, has_prev, 2) + [
            pl.BlockSpec((1, 1, d), lambda p, i: (p, 0, 0)),
            pl.BlockSpec((1, d, d), lambda p, i: (p, 0, 0)),
        ],
        out_specs=pl.BlockSpec((1, tm, d), lambda p, i: (p, i, 0)),
        out_shape=jax.ShapeDtypeStruct((3, n, d), F32),
        compiler_params=_params("arbitrary", "arbitrary"),
        name="rwkv_rkv",
    )(h, prev if has_prev else h, mu3, w3)


def _lora_body(h_ref, aux_ref, mu_ref, w0_ref, a0_ref, w1_ref, w2_ref, a1_ref, a2_ref, g1_ref, g2_ref,
               decay_ref, a_ref, g_ref, *, tm, seq_len, has_prev):
    i = pl.program_id(0)
    h, prev = _token_shift(h_ref, None if has_prev else aux_ref, aux_ref if has_prev else None, i, tm, seq_len)
    xx = prev - h
    xw = (h + xx * mu_ref[0]).astype(BF16)
    xa = (h + xx * mu_ref[1]).astype(BF16)
    xg = (h + xx * mu_ref[2]).astype(BF16)
    lw = _dot(jnp.tanh(_dot(xw, w1_ref[...])).astype(BF16), w2_ref[...])
    w = -_softplus(-(w0_ref[...] + lw)) - 0.5
    decay_ref[...] = -jnp.exp(w)
    la = _dot(_dot(xa, a1_ref[...]).astype(BF16), a2_ref[...])
    a_ref[...] = _sigmoid(a0_ref[...] + la)
    g_ref[...] = _dot(_sigmoid(_dot(xg, g1_ref[...])).astype(BF16), g2_ref[...])


def _rwkv_lora(h, prev, mu3, w0, a0, w1, w2, a1, a2, g1, g2, tm, seq_len):
    n, d = h.shape
    has_prev = prev is not None
    full = lambda arr: pl.BlockSpec(arr.shape, lambda i, nd=arr.ndim: (0,) * nd)
    row = pl.BlockSpec((tm, d), lambda i: (i, 0))
    out = jax.ShapeDtypeStruct((n, d), F32)
    return pl.pallas_call(
        functools.partial(_lora_body, tm=tm, seq_len=seq_len, has_prev=has_prev),
        grid=(n // tm,),
        in_specs=_shift_specs(tm, d, has_prev, 1) + [full(mu3), full(w0), full(a0), full(w1), full(w2),
                                                     full(a1), full(a2), full(g1), full(g2)],
        out_specs=[row, row, row],
        out_shape=[out, out, out],
        compiler_params=_params("parallel"),
        name="rwkv_lora",
    )(h, prev if has_prev else h, mu3, w0, a0, w1, w2, a1, a2, g1, g2)


PAIR_GROUP = 4


def _rows8(x):
    if x.shape[0] >= SUBLANES:
        return x
    return jnp.broadcast_to(x[0:1], (SUBLANES, x.shape[1]))


def _segsum(x, ones_bd):
    m = x.shape[0]
    return _dot_hilo(_rows8(x), ones_bd)[:m]


def _wkv_body(r_ref, k_ref, v_ref, d_ref, a_ref, g_ref, kkw_ref, kaw_ref, rkw_ref, lnw_ref, lnb_ref, s0_ref,
              o_ref, st_ref,
              s_scr, kk_scr, kp_scr, ka_scr, dr_scr, c1_scr, c2_scr, dd_scr, vv_scr, y_scr, *, tc):
    t_chunk = pl.program_id(1)
    unroll = min(tc, SUBLANES)
    ones_bd = _block_ones(LANES, RWKV_HEAD)
    rows = lax.broadcasted_iota(jnp.int32, (RWKV_HEAD, LANES), 0)
    lanes = lax.broadcasted_iota(jnp.int32, (RWKV_HEAD, LANES), 1)
    diag = (lanes % RWKV_HEAD) == rows

    @pl.when(t_chunk == 0)
    def _load_state():
        for p in range(HEAD_PAIRS):
            s_scr[p] = jnp.concatenate([s0_ref[0, 2 * p], s0_ref[0, 2 * p + 1]], axis=1)

    for p in range(HEAD_PAIRS):
        sl = slice(p * LANES, (p + 1) * LANES)
        r = r_ref[0, :, sl]
        k = k_ref[0, :, sl]
        a = a_ref[0, :, sl]
        kk = k * kkw_ref[:, sl]
        norm = jnp.sqrt(_segsum(kk * kk, ones_bd))
        kk = kk / jnp.maximum(norm, 1e-12)
        kp = k * (1.0 + (a - 1.0) * kaw_ref[:, sl])
        ka = kk * a
        d = jnp.exp(d_ref[0, :, sl])
        kk_scr[p] = _rows8(kk)
        kp_scr[p] = _rows8(kp)
        ka_scr[p] = _rows8(ka)
        dr_scr[p] = _rows8(d * r)
        c1_scr[p] = _rows8(_segsum(ka * r, ones_bd))
        c2_scr[p] = _rows8(_segsum(kp * r, ones_bd))
        dd_scr[p] = _rows8(d)
        vv_scr[p] = _rows8(v_ref[0, :, sl])

    blocks = tc // unroll

    def group_steps(idx, carry):
        grp = idx // blocks
        t0 = pl.multiple_of((idx % blocks) * unroll, unroll) if blocks > 1 else 0
        for pp in range(PAIR_GROUP):
            p = grp * PAIR_GROUP + pp
            rows_of = lambda ref: ref[p, pl.ds(t0, SUBLANES), :]
            kk8, dr8, ka8, kp8, c18, c28, dd8, vv8 = [rows_of(ref) for ref in (
                kk_scr, dr_scr, ka_scr, kp_scr, c1_scr, c2_scr, dd_scr, vv_scr)]
            s = s_scr[p]
            y_rows = []
            for j in range(unroll):
                row = lambda x: x[j:j + 1, :]
                s_kk = _dot_hilo(s * row(kk8), ones_bd)
                y_part = _dot_hilo(s * row(dr8), ones_bd)
                v_col = _dot_hilo(jnp.where(diag, row(vv8), 0.0), ones_bd)
                s = s * row(dd8) - s_kk * row(ka8) + v_col * row(kp8)
                y_col = y_part - s_kk * row(c18) + v_col * row(c28)
                y_rows.append(jnp.sum(jnp.where(diag, y_col, 0.0), axis=0, keepdims=True))
            s_scr[p] = s
            y_scr[p, pl.ds(t0, SUBLANES), :] = _rows8(jnp.concatenate(y_rows, axis=0))
        return carry
    lax.fori_loop(0, (HEAD_PAIRS // PAIR_GROUP) * blocks, group_steps, 0)

    for p in range(HEAD_PAIRS):
        sl = slice(p * LANES, (p + 1) * LANES)
        y = y_scr[p, 0:tc, :]
        mean = _segsum(y, ones_bd) * (1.0 / RWKV_HEAD)
        yc = y - mean
        var = _segsum(yc * yc, ones_bd) * (1.0 / RWKV_HEAD)
        yn = yc * lax.rsqrt(var + GN_EPS) * lnw_ref[:, sl] + lnb_ref[:, sl]
        bonus = _segsum(r_ref[0, :, sl] * kp_scr[p, 0:tc, :] * rkw_ref[:, sl], ones_bd) * v_ref[0, :, sl]
        o_ref[0, :, sl] = (yn + bonus) * g_ref[0, :, sl]

    @pl.when(t_chunk == pl.num_programs(1) - 1)
    def _store_state():
        for p in range(HEAD_PAIRS):
            s = s_scr[p]
            st_ref[0, 2 * p] = s[:, :RWKV_HEAD]
            st_ref[0, 2 * p + 1] = s[:, RWKV_HEAD:]


def _wkv(rkv, decay, a, g, kkw, kaw, rkw, lnw, lnb, s0, batch, seq_len, tc):
    d = D_MODEL
    rkv4 = rkv.reshape(3, batch, seq_len, d)
    seq3 = lambda x: x.reshape(batch, seq_len, d)
    tok = lambda which: pl.BlockSpec((None, 1, tc, d), lambda b, t, which=which: (which, b, t, 0))
    tok3 = pl.BlockSpec((1, tc, d), lambda b, t: (b, t, 0))
    vec = pl.BlockSpec((1, d), lambda b, t: (0, 0))
    state = pl.BlockSpec((1, RWKV_HEADS, RWKV_HEAD, RWKV_HEAD), lambda b, t: (b, 0, 0, 0))
    chunk = pltpu.VMEM((HEAD_PAIRS, max(tc, SUBLANES), LANES), F32)
    o, st = pl.pallas_call(
        functools.partial(_wkv_body, tc=tc),
        grid=(batch, seq_len // tc),
        in_specs=[tok(0), tok(1), tok(2), tok3, tok3, tok3, vec, vec, vec, vec, vec, state],
        out_specs=[tok3, state],
        out_shape=[jax.ShapeDtypeStruct((batch, seq_len, d), F32),
                   jax.ShapeDtypeStruct((batch, RWKV_HEADS, RWKV_HEAD, RWKV_HEAD), F32)],
        scratch_shapes=[pltpu.VMEM((HEAD_PAIRS, RWKV_HEAD, LANES), F32)] + [chunk] * 9,
        compiler_params=_params("parallel", "arbitrary"),
        name="wkv",
    )(rkv4, rkv4, rkv4, seq3(decay), seq3(a), seq3(g), kkw, kaw, rkw, lnw, lnb, s0)
    return o.reshape(batch * seq_len, d), st


WKV_CHUNK = 64
WKV_STEP_PAIRS = 8
WKV_STEP_LANES = WKV_STEP_PAIRS * LANES


def _dot_nt(a, b):
    return lax.dot_general(a, b, (((1,), (1,)), ((), ())), preferred_element_type=F32)


def _split3(x):
    hi = x.astype(BF16)
    r1 = x - hi.astype(F32)
    mid = r1.astype(BF16)
    lo = (r1 - mid.astype(F32)).astype(BF16)
    return hi, mid, lo


def _wkv_chunk_body(r_ref, k_ref, v_ref, ld_ref, a_ref, g_ref, kkw_ref, kaw_ref, rkw_ref, lnw_ref, lnb_ref,
                    o_ref, st_ref, s_scr):
    t_chunk = pl.program_id(2)
    c = WKV_CHUNK
    ones_bd = _block_ones(LANES, RWKV_HEAD)
    row = lax.broadcasted_iota(jnp.int32, (LANES, LANES), 0)
    col = lax.broadcasted_iota(jnp.int32, (LANES, LANES), 1)
    strict = col < row
    incl = col <= row
    eye = jnp.where(row == col, 1.0, 0.0)
    ltri = jnp.where(incl, 1.0, 0.0).astype(BF16)
    head0 = lax.broadcasted_iota(jnp.int32, (c, LANES), 1) < RWKV_HEAD
    stack = lambda x: jnp.concatenate([jnp.where(head0, x, 0.0), jnp.where(head0, 0.0, x)], axis=0)

    @pl.when(t_chunk == 0)
    def _zero_state():
        s_scr[...] = jnp.zeros_like(s_scr)

    pairs = range(WKV_STEP_PAIRS)
    lanes_of = [slice(pp * LANES, (pp + 1) * LANES) for pp in pairs]
    each = lambda fn, *lists: [fn(*args) for args in zip(*lists)]
    bf = lambda x: x.astype(BF16)
    r = [r_ref[0, :, sl] for sl in lanes_of]
    k = [k_ref[0, :, sl] for sl in lanes_of]
    v = [v_ref[0, :, sl] for sl in lanes_of]
    ld = [ld_ref[0, :, sl] for sl in lanes_of]
    a = [a_ref[0, :, sl] for sl in lanes_of]
    kk_raw = each(lambda x, sl: x * kkw_ref[:, sl], k, lanes_of)
    kk_norm = each(lambda x: jnp.sqrt(_segsum(x * x, ones_bd)), kk_raw)
    kk = each(lambda x, n: x / jnp.maximum(n, 1e-12), kk_raw, kk_norm)
    kp = each(lambda x, y, sl: x * (1.0 + (y - 1.0) * kaw_ref[:, sl]), k, a, lanes_of)
    ka = each(lambda x, y: x * y, kk, a)
    ld_pieces = each(lambda x: _split3(jnp.concatenate([x, jnp.zeros_like(x)], axis=0)), ld)
    cum2 = [_dot(ltri, p3[0]) + _dot(ltri, p3[1]) + _dot(ltri, p3[2]) for p3 in ld_pieces]
    cum = [x[:c] for x in cum2]
    gam_end = [jnp.exp(x[c:]) for x in cum2]
    gam_inv = [jnp.exp(-x) for x in cum]
    at = each(lambda x, y: x * y, ka, gam_inv)
    kt = each(lambda x, y: x * y, kp, gam_inv)
    x_st = each(lambda kk_, r_, cum_, ld_: bf(jnp.concatenate(
        [stack(kk_ * jnp.exp(cum_ - ld_)), stack(r_ * jnp.exp(cum_))], axis=0)), kk, r, cum, ld)
    y_st = each(lambda x, y: bf(jnp.concatenate([stack(x), stack(y)], axis=0)), at, kt)
    gram = each(_dot_nt, x_st, y_st)
    m_a = [jnp.where(strict, x[:LANES, :LANES], 0.0) for x in gram]
    m_k = [bf(jnp.where(strict, x[:LANES, LANES:], 0.0)) for x in gram]
    n_a = [bf(jnp.where(incl, x[LANES:, :LANES], 0.0)) for x in gram]
    n_k = [bf(jnp.where(incl, x[LANES:, LANES:], 0.0)) for x in gram]
    power = [-x for x in m_a]
    inv = [eye + x for x in power]
    for _ in range(5):
        power = [_dot(bf(x), bf(x)) for x in power]
        inv = each(lambda t, p: t + _dot(bf(t), bf(p)), inv, power)
    inv = [bf(x) for x in inv]
    v_st = [stack(x) for x in v]
    v_stb = [bf(x) for x in v_st]
    v_stt = [bf(x.T) for x in v_st]
    at_end = each(lambda x, y: bf(stack(x * y)), at, gam_end)
    kt_end = each(lambda x, y: bf(stack(x * y)), kt, gam_end)
    s_from_v = each(_dot, v_stt, kt_end)
    m_kv = each(_dot, m_k, v_stb)
    n_kv = each(_dot, n_k, v_stb)
    s = [s_scr[pp] for pp in pairs]
    proj = each(lambda x, s_: _dot_nt(x, bf(s_)), x_st, s)
    u = each(lambda t, p, mv: -_dot(t, bf(p[:LANES] + mv)), inv, proj, m_kv)
    y2 = each(lambda p, n, u_, nv: p[LANES:] + _dot(n, bf(u_)) + nv, proj, n_a, u, n_kv)
    s_from_u = each(lambda u_, x: _dot(bf(u_.T), x), u, at_end)
    for pp in pairs:
        s_scr[pp] = s[pp] * jnp.concatenate([gam_end[pp], gam_end[pp]], axis=0) + s_from_u[pp] + s_from_v[pp]
    y = [x[:c] + x[c:] for x in y2]
    mean = [_segsum(x, ones_bd) * (1.0 / RWKV_HEAD) for x in y]
    yc = each(lambda x, m: x - m, y, mean)
    var = [_segsum(x * x, ones_bd) * (1.0 / RWKV_HEAD) for x in yc]
    bonus_dot = each(lambda r_, kp_, sl: _segsum(r_ * kp_ * rkw_ref[:, sl], ones_bd), r, kp, lanes_of)
    for pp in pairs:
        sl = lanes_of[pp]
        yn = yc[pp] * lax.rsqrt(var[pp] + GN_EPS) * lnw_ref[:, sl] + lnb_ref[:, sl]
        o_ref[0, :, sl] = (yn + bonus_dot[pp] * v[pp]) * g_ref[0, :, sl]

    @pl.when(t_chunk == pl.num_programs(2) - 1)
    def _store_state():
        for pp in range(WKV_STEP_PAIRS):
            s = s_scr[pp]
            st_ref[0, 2 * pp] = s[:RWKV_HEAD, :RWKV_HEAD]
            st_ref[0, 2 * pp + 1] = s[RWKV_HEAD:, RWKV_HEAD:]


def _wkv_chunked(rkv, logd, a, g, kkw, kaw, rkw, lnw, lnb, batch, seq_len):
    d = D_MODEL
    c = WKV_CHUNK
    w = WKV_STEP_LANES
    rkv4 = rkv.reshape(3, batch, seq_len, d)
    seq3 = lambda x: x.reshape(batch, seq_len, d)
    tok = lambda which: pl.BlockSpec((None, 1, c, w), lambda b, p, t, which=which: (which, b, t, p))
    tok3 = pl.BlockSpec((1, c, w), lambda b, p, t: (b, t, p))
    vec = pl.BlockSpec((1, w), lambda b, p, t: (0, p))
    state = pl.BlockSpec((1, 2 * WKV_STEP_PAIRS, RWKV_HEAD, RWKV_HEAD), lambda b, p, t: (b, p, 0, 0))
    o, st = pl.pallas_call(
        _wkv_chunk_body,
        grid=(batch, d // w, seq_len // c),
        in_specs=[tok(0), tok(1), tok(2), tok3, tok3, tok3, vec, vec, vec, vec, vec],
        out_specs=[tok3, state],
        out_shape=[jax.ShapeDtypeStruct((batch, seq_len, d), F32),
                   jax.ShapeDtypeStruct((batch, RWKV_HEADS, RWKV_HEAD, RWKV_HEAD), F32)],
        scratch_shapes=[pltpu.VMEM((WKV_STEP_PAIRS, LANES, LANES), F32)],
        compiler_params=_params("parallel", "parallel", "arbitrary"),
        name="wkv_chunked",
    )(rkv4, rkv4, rkv4, seq3(logd), seq3(a), seq3(g), kkw, kaw, rkw, lnw, lnb)
    return o.reshape(batch * seq_len, d), st


SB_STEP_HEADS = 8
SB_STEP_LANES = SB_STEP_HEADS * SB_HEAD_DIM


def _sb_prompt_body(q_ref, k_ref, v_ref, b_ref, o_ref, acc_scr, tail_scr):
    qi = pl.program_id(2)
    scale = SB_HEAD_DIM ** -0.5
    t_idx = lax.broadcasted_iota(jnp.int32, (Q_BLOCK, Q_BLOCK), 0)
    s_idx = lax.broadcasted_iota(jnp.int32, (Q_BLOCK, Q_BLOCK), 1)
    later = jnp.where(t_idx > s_idx, 1.0, 0.0).astype(BF16)
    later_and_ones = jnp.concatenate([later, jnp.ones((Q_BLOCK, Q_BLOCK), BF16)], axis=1)

    acc_scr[...] = jnp.zeros_like(acc_scr)
    tail_scr[...] = jnp.zeros_like(tail_scr)

    def body(j, carry):
        kb = qi - j
        start = pl.multiple_of(kb * Q_BLOCK, Q_BLOCK)
        mask = (kb * Q_BLOCK + s_idx) < (qi * Q_BLOCK + t_idx)
        heads = range(SB_STEP_HEADS)
        lanes_of = [slice(hh * SB_HEAD_DIM, (hh + 1) * SB_HEAD_DIM) for hh in heads]
        z = [_dot_nt(q_ref[0, :, sl].astype(BF16), k_ref[0, pl.ds(start, Q_BLOCK), sl].astype(BF16)) * scale
             + b_ref[:, sl] for sl in lanes_of]
        sp = [_softplus(x) for x in z]
        log_not = [jnp.where(mask, -x, 0.0) for x in sp]
        hi = [x.astype(BF16) for x in log_not]
        lo = [(x - h.astype(F32)).astype(BF16) for x, h in zip(log_not, hi)]
        sums_hi = [_dot(x, later_and_ones) for x in hi]
        sums_lo = [_dot(x, later_and_ones) for x in lo]
        sums = [x + y for x, y in zip(sums_hi, sums_lo)]
        tail = [tail_scr[hh] for hh in heads]
        p = [jnp.where(mask, jnp.exp(z[hh] - sp[hh] + sums[hh][:, :Q_BLOCK] + tail[hh]), 0.0).astype(BF16)
             for hh in heads]
        pv = [_dot(p[hh], v_ref[0, pl.ds(start, Q_BLOCK), lanes_of[hh]].astype(BF16)) for hh in heads]
        for hh in heads:
            acc_scr[hh] += pv[hh]
            tail_scr[hh] = tail[hh] + sums[hh][:, Q_BLOCK:]
        return carry

    lax.fori_loop(0, qi + 1, body, 0)
    for hh in range(SB_STEP_HEADS):
        o_ref[0, :, hh * SB_HEAD_DIM:(hh + 1) * SB_HEAD_DIM] = acc_scr[hh]


def _sb_prompt(q, k, v, bias_cols, batch, seq_len):
    d = D_MODEL
    w = SB_STEP_LANES
    seq3 = lambda x: x.reshape(batch, seq_len, d)
    qspec = pl.BlockSpec((1, Q_BLOCK, w), lambda b, h, i: (b, i, h))
    kspec = pl.BlockSpec((1, seq_len, w), lambda b, h, i: (b, 0, h))
    vspec = kspec
    block_state = pltpu.VMEM((SB_STEP_HEADS, Q_BLOCK, SB_HEAD_DIM), F32)
    out = pl.pallas_call(
        _sb_prompt_body,
        grid=(batch, d // w, seq_len // Q_BLOCK),
        in_specs=[qspec, kspec, vspec, pl.BlockSpec((1, w), lambda b, h, i: (0, h))],
        out_specs=pl.BlockSpec((1, Q_BLOCK, w), lambda b, h, i: (b, i, h)),
        out_shape=jax.ShapeDtypeStruct((batch, seq_len, d), F32),
        scratch_shapes=[block_state, block_state],
        compiler_params=_params("parallel", "parallel", "arbitrary"),
        name="sb_prompt",
    )(seq3(q), seq3(k), seq3(v), bias_cols)
    return out.reshape(batch * seq_len, d)


PAGE_ROWS = PAGE_SIZE * SB_HEADS


SB_STEP_SEQS = 2


def _sb_sample_body(pt_ref, q_ref, kn_ref, vn_ref, *refs):
    n = SB_STEP_SEQS
    kc_refs, vc_refs = refs[:n], refs[n:2 * n]
    b_ref, col_ref, exp_ref, o_ref, acc_scr, tail_scr = refs[2 * n:]
    seqs = range(n)
    step = pl.program_id(1)
    n_steps = pl.num_programs(1)
    scale = SB_HEAD_DIM ** -0.5
    bias = b_ref[...]
    head = lax.broadcasted_iota(jnp.int32, (SB_HEADS, PAGE_ROWS), 0)
    own_head = lax.broadcasted_iota(jnp.int32, (SB_HEADS, PAGE_ROWS), 1) % SB_HEADS == head
    r_idx = lax.broadcasted_iota(jnp.int32, (PAGE_SIZE, PAGE_SIZE), 0)
    c_idx = lax.broadcasted_iota(jnp.int32, (PAGE_SIZE, PAGE_SIZE), 1)
    later = jnp.where(r_idx > c_idx, 1.0, 0.0).astype(BF16)
    ones = jnp.ones((PAGE_SIZE, PAGE_SIZE), BF16)

    @pl.when(step == 0)
    def _init():
        n_past = n_steps * PAGE_SIZE
        new_is_before = jnp.full((SB_HEADS, LANES), n_past, jnp.int32) < n_past
        for i in seqs:
            z_new = jnp.sum(q_ref[i] * kn_ref[i], axis=-1, keepdims=True) * scale + bias
            sp_new = _softplus(z_new)
            tail_scr[i] = jnp.where(new_is_before, -sp_new, 0.0)
            acc_scr[i] = jnp.where(new_is_before, jnp.exp(z_new - sp_new), 0.0) * vn_ref[i]

    z_all = [_dot_nt(q_ref[i].astype(BF16), kc_refs[i][0].astype(BF16)) for i in seqs]
    z_own = [_split3(jnp.where(own_head, x, 0.0)) for x in z_all]
    collapse = col_ref[...]
    z = [(_dot(p3[0], collapse) + _dot(p3[1], collapse) + _dot(p3[2], collapse)) * scale + bias for p3 in z_own]
    sp = [_softplus(x) for x in z]
    hi = [(-x).astype(BF16) for x in sp]
    lo = [(-x - h.astype(F32)).astype(BF16) for x, h in zip(sp, hi)]
    after = [_dot(h, later) + _dot(l, later) for h, l in zip(hi, lo)]
    total = [_dot(h, ones) + _dot(l, ones) for h, l in zip(hi, lo)]
    tail = [tail_scr[i] for i in seqs]
    p = [jnp.exp(z[i] - sp[i] + after[i] + tail[i]).astype(BF16) for i in seqs]
    p_rows = [jnp.where(own_head, _dot(x, exp_ref[...]), 0.0).astype(BF16) for x in p]
    pv = [_dot(p_rows[i], vc_refs[i][0].astype(BF16)) for i in seqs]
    for i in seqs:
        acc_scr[i] += pv[i]
        tail_scr[i] = tail[i] + total[i]

    @pl.when(step == n_steps - 1)
    def _finish():
        o_ref[...] = acc_scr[...]


def _sb_sample(q, k_new, v_new, cache_k, cache_v, page_table, bias):
    batch, d = q.shape
    n_pages = page_table.shape[1]
    n_phys = cache_k.shape[0]
    pages = lambda c: c.reshape(n_phys, PAGE_ROWS, SB_HEAD_DIM)
    row_token = lax.broadcasted_iota(jnp.int32, (PAGE_ROWS, PAGE_SIZE), 0) // SB_HEADS
    collapse = (row_token == lax.broadcasted_iota(jnp.int32, (PAGE_ROWS, PAGE_SIZE), 1)).astype(BF16)
    bias_lanes = jnp.broadcast_to(bias[:, None], (SB_HEADS, LANES)).astype(F32)
    n = SB_STEP_SEQS
    tok = pl.BlockSpec((n, SB_HEADS, SB_HEAD_DIM), lambda b, s, pt: (b, 0, 0))
    page = lambda i: pl.BlockSpec((1, PAGE_ROWS, SB_HEAD_DIM),
                                  lambda b, s, pt, i=i: (pt[b * n + i, n_pages - 1 - s], 0, 0))
    const = lambda shape: pl.BlockSpec(shape, lambda b, s, pt: (0, 0))
    heads3 = lambda x: x.reshape(batch, SB_HEADS, SB_HEAD_DIM)
    per_seq = pltpu.VMEM((n, SB_HEADS, SB_HEAD_DIM), F32)
    out = pl.pallas_call(
        _sb_sample_body,
        grid_spec=pltpu.PrefetchScalarGridSpec(
            num_scalar_prefetch=1,
            grid=(batch // n, n_pages),
            in_specs=[tok, tok, tok] + [page(i) for i in range(n)] * 2 + [
                const((SB_HEADS, LANES)), const((PAGE_ROWS, PAGE_SIZE)), const((PAGE_SIZE, PAGE_ROWS))],
            out_specs=tok,
            scratch_shapes=[per_seq, per_seq],
        ),
        out_shape=jax.ShapeDtypeStruct((batch, SB_HEADS, SB_HEAD_DIM), F32),
        compiler_params=_params("parallel", "arbitrary"),
        name="sb_sample",
    )(page_table, heads3(q), heads3(k_new), heads3(v_new), *([pages(cache_k)] * n), *([pages(cache_v)] * n),
      bias_lanes, collapse, collapse.T)
    return out.reshape(batch, d)


ROUTER_OFFSET = N_GROUPS


def _router_body(x_ref, w_ref, b_ref, gate_ref, route_ref):
    logits = jnp.dot(x_ref[...], w_ref[...], preferred_element_type=F32,
                     precision=lax.Precision.HIGHEST) + b_ref[...]
    lane = lax.broadcasted_iota(jnp.int32, logits.shape, 1)
    neg = -jnp.inf
    first_lane_of = lambda hit: jnp.min(jnp.where(hit, lane, LANES), axis=-1, keepdims=True)
    is_group = lane < N_GROUPS
    glog = jnp.where(is_group, logits, neg)
    gmax = jnp.max(glog, axis=-1, keepdims=True)
    g_sel = first_lane_of(glog == gmax)
    p_group = 1.0 / jnp.sum(jnp.exp(glog - gmax), axis=-1, keepdims=True)
    in_group = ((lane - ROUTER_OFFSET) // EXPERTS_PER_GROUP == g_sel) & (lane >= ROUTER_OFFSET) \
        & (lane < ROUTER_OFFSET + N_EXPERTS)
    elog = jnp.where(in_group, logits, neg)
    top1 = jnp.max(elog, axis=-1, keepdims=True)
    lane1 = first_lane_of(elog == top1)
    rest = jnp.where(lane == lane1, neg, elog)
    top2 = jnp.max(rest, axis=-1, keepdims=True)
    lane2 = first_lane_of(rest == top2)
    e2 = jnp.exp(top2 - top1)
    w1 = p_group / (1.0 + e2)
    w2 = p_group * e2 / (1.0 + e2)
    gate_ref[...] = jnp.where(lane == lane1, w1, 0.0) + jnp.where(lane == lane2, w2, 0.0)
    route_ref[...] = (jnp.where(lane == 0, (lane1 - ROUTER_OFFSET).astype(F32), 0.0)
                      + jnp.where(lane == 1, (lane2 - ROUTER_OFFSET).astype(F32), 0.0)
                      + jnp.where(lane == 2, w1, 0.0) + jnp.where(lane == 3, w2, 0.0))


def _router(x, w_router, b_router, tm):
    n, d = x.shape
    out = jax.ShapeDtypeStruct((n, LANES), F32)
    row = pl.BlockSpec((tm, LANES), lambda i: (i, 0))
    return pl.pallas_call(
        _router_body,
        grid=(n // tm,),
        in_specs=[pl.BlockSpec((tm, d), lambda i: (i, 0)),
                  pl.BlockSpec((d, LANES), lambda i: (0, 0)),
                  pl.BlockSpec((1, LANES), lambda i: (0, 0))],
        out_specs=[row, row],
        out_shape=[out, out],
        compiler_params=_params("parallel"),
        name="moe_router",
    )(x, w_router, b_router)


def _moe_body(x_ref, gate_ref, res_ref, wg_ref, wu_ref, wd_ref, o_ref, xb_scr):
    e = pl.program_id(1)

    @pl.when(e == 0)
    def _init():
        xb_scr[...] = x_ref[...].astype(BF16)
        o_ref[...] = res_ref[...]

    xb = xb_scr[...]
    hg = _dot(xb, wg_ref[0].astype(BF16))
    hu = _dot(xb, wu_ref[0].astype(BF16))
    gate = gate_ref[...]
    lane = lax.broadcasted_iota(jnp.int32, gate.shape, 1)
    ge = jnp.sum(jnp.where(lane == e + ROUTER_OFFSET, gate, 0.0), axis=-1, keepdims=True)
    act = (hg * _sigmoid(hg)) * hu * ge
    o_ref[...] += _dot(act.astype(BF16), wd_ref[0].astype(BF16))


def _moe_dense(x, gate, residual, wg, wu, wd, tm):
    n, d = x.shape
    row = lambda i, e: (i, 0)
    return pl.pallas_call(
        _moe_body,
        grid=(n // tm, N_EXPERTS),
        in_specs=[pl.BlockSpec((tm, d), row), pl.BlockSpec((tm, LANES), row), pl.BlockSpec((tm, d), row),
                  pl.BlockSpec((1, d, D_EXPERT), lambda i, e: (e, 0, 0)),
                  pl.BlockSpec((1, d, D_EXPERT), lambda i, e: (e, 0, 0)),
                  pl.BlockSpec((1, D_EXPERT, d), lambda i, e: (e, 0, 0))],
        out_specs=pl.BlockSpec((tm, d), row),
        out_shape=jax.ShapeDtypeStruct((n, d), F32),
        scratch_shapes=[pltpu.VMEM((tm, d), BF16)],
        compiler_params=_params("parallel", "arbitrary"),
        name="moe_experts",
    )(x, gate, residual, wg, wu, wd)


MOE_TILE = 256
COMBINE_TILE = 256


def _row_copy(src_hbm, index_ref, base, buf, sem):
    def copy(r):
        return pltpu.make_async_copy(src_hbm.at[pl.ds(index_ref[base + r], 1)], buf.at[pl.ds(r, 1)], sem)
    return copy


def _for_rows(n_rows, fn):
    def body(r, carry):
        fn(r)
        return carry
    lax.fori_loop(0, n_rows, body, 0, unroll=8)


def _gather_pipeline(step, n_steps, copies_of, n_rows):
    slot = step % 2

    @pl.when(step == 0)
    def _prime():
        _for_rows(n_rows, lambda r: copies_of(0, 0)(r).start())

    @pl.when(step + 1 < n_steps)
    def _prefetch():
        _for_rows(n_rows, lambda r: copies_of(step + 1, 1 - slot)(r).start())

    _for_rows(n_rows, lambda r: copies_of(step, slot)(r).wait())
    return slot


def _moe_sparse_body(tile_expert_ref, row_token_ref, n_used_ref, h_hbm, w_ref, wg_ref, wu_ref, wd_ref, y_ref,
                     xbuf, sem, wg_bf, wu_bf, wd_bf):
    t = pl.program_id(0)
    copies_of = lambda tile, s: _row_copy(h_hbm, row_token_ref, tile * MOE_TILE, xbuf.at[s], sem.at[s])
    slot = _gather_pipeline(t, pl.num_programs(0), copies_of, MOE_TILE)

    @pl.when(jnp.logical_or(t == 0, tile_expert_ref[t] != tile_expert_ref[jnp.maximum(t - 1, 0)]))
    def _new_expert():
        wg_bf[...] = wg_ref[0].astype(BF16)
        wu_bf[...] = wu_ref[0].astype(BF16)
        wd_bf[...] = wd_ref[0].astype(BF16)

    @pl.when(t < n_used_ref[0])
    def _compute():
        x = xbuf[slot].astype(BF16)
        hg = _dot(x, wg_bf[...])
        hu = _dot(x, wu_bf[...])
        w = w_ref[...]
        act = (hg * _sigmoid(hg)) * hu * jnp.concatenate([w] * (D_EXPERT // LANES), axis=1)
        y_ref[...] = _dot(act.astype(BF16), wd_bf[...])

    @pl.when(t >= n_used_ref[0])
    def _unused_tile():
        y_ref[...] = jnp.zeros_like(y_ref)


def _moe_combine_body(pos_ref, x_ref, y_hbm, o_ref, ybuf, sem):
    i = pl.program_id(0)
    tm = x_ref.shape[0]
    copies_of = lambda tile, s: _row_copy(y_hbm, pos_ref, tile * 2 * tm, ybuf.at[s], sem.at[s])
    slot = _gather_pipeline(i, pl.num_programs(0), copies_of, 2 * tm)
    rows = ybuf[slot]
    o_ref[...] = x_ref[...] + rows[:tm] + rows[tm:]


def _route_plan(route, tile, tm):
    n = route.shape[0]
    a = 2 * n
    e = route[:, :2].astype(jnp.int32).reshape(a)
    wts = route[:, 2:4].reshape(a)
    onehot = (e[:, None] == jnp.arange(N_EXPERTS, dtype=jnp.int32)[None, :]).astype(jnp.int32)
    rank = jnp.sum((jnp.cumsum(onehot, axis=0) - onehot) * onehot, axis=1)
    counts = jnp.sum(onehot, axis=0)
    padded = (counts + tile - 1) // tile * tile
    ends = jnp.cumsum(padded)
    offs = ends - padded
    first = jnp.cumsum(counts) - counts
    pos = offs[e] + rank
    sorted_assign = jnp.sort(e * a + jnp.arange(a, dtype=jnp.int32)) % a
    p_rows = a + N_EXPERTS * tile
    n_tiles = p_rows // tile
    tile_expert = jnp.minimum(jnp.searchsorted(ends, jnp.arange(n_tiles, dtype=jnp.int32) * tile, side="right"),
                              N_EXPERTS - 1).astype(jnp.int32)
    row = jnp.arange(p_rows, dtype=jnp.int32)
    row_expert = tile_expert[row // tile]
    within = row - offs[row_expert]
    valid = (within < counts[row_expert]) & (row < ends[-1])
    src = jnp.where(valid, sorted_assign[jnp.clip(first[row_expert] + within, 0, a - 1)], 0)
    row_token = (src // 2).astype(jnp.int32)
    row_w = jnp.where(valid, wts[src], 0.0)
    n_used = (ends[-1] // tile).astype(jnp.int32).reshape(1)
    pos_tiled = pos.reshape(n // tm, tm, 2).transpose(0, 2, 1).reshape(a).astype(jnp.int32)
    return tile_expert, row_token, n_used, jnp.broadcast_to(row_w[:, None], (p_rows, LANES)), pos_tiled


def _moe_sparse(h, route, residual, wg, wu, wd):
    n, d = h.shape
    tile, tm = MOE_TILE, COMBINE_TILE
    tile_expert, row_token, n_used, row_w, pos_tiled = _route_plan(route, tile, tm)
    p_rows = row_w.shape[0]
    y = pl.pallas_call(
        _moe_sparse_body,
        grid_spec=pltpu.PrefetchScalarGridSpec(
            num_scalar_prefetch=3,
            grid=(p_rows // tile,),
            in_specs=[pl.BlockSpec(memory_space=pl.ANY),
                      pl.BlockSpec((tile, LANES), lambda t, te, rt, nu: (t, 0)),
                      pl.BlockSpec((1, d, D_EXPERT), lambda t, te, rt, nu: (te[t], 0, 0)),
                      pl.BlockSpec((1, d, D_EXPERT), lambda t, te, rt, nu: (te[t], 0, 0)),
                      pl.BlockSpec((1, D_EXPERT, d), lambda t, te, rt, nu: (te[t], 0, 0))],
            out_specs=pl.BlockSpec((tile, d), lambda t, te, rt, nu: (t, 0)),
            scratch_shapes=[pltpu.VMEM((2, tile, d), F32), pltpu.SemaphoreType.DMA((2,)),
                            pltpu.VMEM((d, D_EXPERT), BF16), pltpu.VMEM((d, D_EXPERT), BF16),
                            pltpu.VMEM((D_EXPERT, d), BF16)],
        ),
        out_shape=jax.ShapeDtypeStruct((p_rows, d), F32),
        compiler_params=_params("arbitrary"),
        name="moe_sparse_experts",
    )(tile_expert, row_token, n_used, h, row_w, wg, wu, wd)
    return pl.pallas_call(
        _moe_combine_body,
        grid_spec=pltpu.PrefetchScalarGridSpec(
            num_scalar_prefetch=1,
            grid=(n // tm,),
            in_specs=[pl.BlockSpec((tm, d), lambda i, pos: (i, 0)), pl.BlockSpec(memory_space=pl.ANY)],
            out_specs=pl.BlockSpec((tm, d), lambda i, pos: (i, 0)),
            scratch_shapes=[pltpu.VMEM((2, 2 * tm, d), F32), pltpu.SemaphoreType.DMA((2,))],
        ),
        out_shape=jax.ShapeDtypeStruct((n, d), F32),
        compiler_params=_params("arbitrary"),
        name="moe_combine",
    )(pos_tiled, residual, y)


def _row_tile(n):
    return 256 if n % 256 == 0 else LANES


def _ffn(x, norm_g, w_router, b_router, wg, wu, wd):
    n = x.shape[0]
    tm = _row_tile(n)
    h = _rmsnorm(x, norm_g, tm)
    gate, route = _router(h, w_router, b_router, tm)
    if n % COMBINE_TILE == 0 and n >= N_EXPERTS * MOE_TILE:
        return _moe_sparse(h, route, x, wg, wu, wd)
    return _moe_dense(h, gate, x, wg, wu, wd, LANES)


def kernel(x_prompt, x_sample, state_tshift, state_wkv, cache_k, cache_v, page_table, norm_mix, norm_ffn, norm_final, rwkv_mu, rwkv_w0, rwkv_w1, rwkv_w2, rwkv_a0, rwkv_a1, rwkv_a2, rwkv_g1, rwkv_g2, rwkv_kk, rwkv_ka, rwkv_rk, rwkv_wr, rwkv_wk, rwkv_wv, rwkv_wo, rwkv_lnw, rwkv_lnb, sb_wqkv, sb_wo, sb_bias, moe_wgroup, moe_bgroup, moe_wexpert, moe_bexpert, moe_wgate, moe_wup, moe_wdown):
    d = D_MODEL
    bp, tp, _ = x_prompt.shape
    bs, ts, _ = x_sample.shape
    xp = x_prompt.reshape(bp * tp, d)
    xs = x_sample.reshape(bs * ts, d)
    row1 = lambda v: v.reshape(1, d)
    pad_cols = lambda w: jnp.pad(w, ((0, 0), (0, LORA_PAD - w.shape[1]))).astype(BF16)
    pad_rows = lambda w: jnp.pad(w, ((0, LORA_PAD - w.shape[0]), (0, 0))).astype(BF16)

    mu = rwkv_mu[0]
    mu_rkv = jnp.stack([mu[0], mu[2], mu[3]]).reshape(3, 1, d)
    mu_lora = jnp.stack([mu[1], mu[4], mu[5]]).reshape(3, 1, d)
    w_rkv = jnp.stack([rwkv_wr[0], rwkv_wk[0], rwkv_wv[0]]).astype(BF16)
    lora = (row1(rwkv_w0[0]), row1(rwkv_a0[0]), pad_cols(rwkv_w1[0]), pad_rows(rwkv_w2[0]),
            pad_cols(rwkv_a1[0]), pad_rows(rwkv_a2[0]), rwkv_g1[0].astype(BF16), rwkv_g2[0].astype(BF16))
    head_vecs = (row1(rwkv_kk[0]), row1(rwkv_ka[0]), row1(rwkv_rk[0]), row1(rwkv_lnw[0]), row1(rwkv_lnb[0]))
    wo = rwkv_wo[0].astype(BF16)

    def rwkv_layer(x, prev, s0, batch, seq_len):
        tm = _row_tile(x.shape[0])
        h = _rmsnorm(x, norm_mix[0], tm)
        rkv = _rwkv_rkv(h, prev, mu_rkv, w_rkv, tm, seq_len)
        logd, a, g = _rwkv_lora(h, prev, mu_lora, *lora, tm, seq_len)
        if s0 is None:
            o, s_final = _wkv_chunked(rkv, logd, a, g, *head_vecs, batch, seq_len)
        else:
            o, s_final = _wkv(rkv, logd, a, g, *head_vecs, s0, batch, seq_len, seq_len)
        return _matmul(o, wo, tm, residual=x), h, s_final

    xp, hp, wkv_p = rwkv_layer(xp, None, None, bp, tp)
    xs, hs, wkv_s = rwkv_layer(xs, state_tshift[0], state_wkv[0], bs, ts)
    shift_p = hp.reshape(bp, tp, d)[:, -1]
    shift_s = hs.reshape(bs, ts, d)[:, -1]

    def ffn(x, i):
        w_router = jnp.pad(jnp.concatenate([moe_wgroup[i], moe_wexpert[i]], axis=1),
                           ((0, 0), (0, LANES - N_GROUPS - N_EXPERTS)))
        b_router = jnp.pad(jnp.concatenate([moe_bgroup[i], moe_bexpert[i]]),
                           (0, LANES - N_GROUPS - N_EXPERTS)).reshape(1, LANES)
        return _ffn(x, norm_ffn[i], w_router, b_router, moe_wgate[i], moe_wup[i], moe_wdown[i])

    xp = ffn(xp, 0)
    xs = ffn(xs, 0)

    w_qkv = jnp.stack(jnp.split(sb_wqkv[0], 3, axis=-1)).astype(BF16)
    w_att_out = sb_wo[0].astype(BF16)
    bias_cols = jnp.repeat(sb_bias[0].astype(F32), SB_HEAD_DIM).reshape(1, d)

    def qkv_proj(x):
        tm = _row_tile(x.shape[0])
        h = _rmsnorm(x, norm_mix[1], tm)
        return [_matmul(h, w_qkv[j], tm) for j in range(3)], tm

    qkv_p, tm_p = qkv_proj(xp)
    att_p = _sb_prompt(*qkv_p, bias_cols, bp, tp)
    xp = _matmul(att_p, w_att_out, tm_p, residual=xp)

    qkv_s, tm_s = qkv_proj(xs)
    att_s = _sb_sample(qkv_s[0], qkv_s[1], qkv_s[2], cache_k[0], cache_v[0], page_table, sb_bias[0])
    xs = _matmul(att_s, w_att_out, tm_s, residual=xs)

    xp = ffn(xp, 1)
    xs = ffn(xs, 1)

    y_prompt = _rmsnorm(xp, norm_final, _row_tile(xp.shape[0])).reshape(bp, tp, d)
    y_sample = _rmsnorm(xs, norm_final, _row_tile(xs.shape[0])).reshape(bs, ts, d)
    heads5 = lambda x, b, t: x.reshape(1, b, t, SB_HEADS, SB_HEAD_DIM)
    return (y_prompt, y_sample,
            shift_p[None], wkv_p[None],
            heads5(qkv_p[1], bp, tp), heads5(qkv_p[2], bp, tp),
            shift_s[None], wkv_s[None],
            heads5(qkv_s[1], bs, ts), heads5(qkv_s[2], bs, ts))
```

```python
import functools

import jax
import jax.numpy as jnp
from jax import lax
from jax.experimental import pallas as pl
from jax.experimental.pallas import tpu as pltpu

F32 = jnp.float32
BF16 = jnp.bfloat16

D_MODEL = 2048
RWKV_HEAD = 64
RWKV_HEADS = D_MODEL // RWKV_HEAD
HEAD_PAIRS = RWKV_HEADS // 2
LORA_PAD = 128
GN_EPS = 64e-5
RMS_EPS = 1e-6
SB_HEADS = 16
SB_HEAD_DIM = D_MODEL // SB_HEADS
Q_BLOCK = 128
PAGE_SIZE = 128
N_GROUPS = 4
EXPERTS_PER_GROUP = 8
N_EXPERTS = N_GROUPS * EXPERTS_PER_GROUP
D_EXPERT = 256
LANES = 128
SUBLANES = 8
VMEM_LIMIT_BYTES = 52 * 1024 * 1024


def _params(*semantics):
    return pltpu.CompilerParams(dimension_semantics=semantics, vmem_limit_bytes=VMEM_LIMIT_BYTES)


def _softplus(x):
    return jnp.maximum(x, 0.0) + jnp.log1p(jnp.exp(-jnp.abs(x)))


def _sigmoid(x):
    return 1.0 / (1.0 + jnp.exp(-x))


def _dot(a, b):
    return jnp.dot(a, b, preferred_element_type=F32)


def _dot_hilo(a, b_bf16):
    hi = a.astype(BF16)
    lo = (a - hi.astype(F32)).astype(BF16)
    return _dot(hi, b_bf16) + _dot(lo, b_bf16)


def _block_ones(n, seg):
    r = lax.broadcasted_iota(jnp.int32, (n, n), 0) // seg
    c = lax.broadcasted_iota(jnp.int32, (n, n), 1) // seg
    return jnp.where(r == c, 1.0, 0.0).astype(BF16)


def _rmsnorm_body(x_ref, g_ref, o_ref):
    x = x_ref[...]
    ms = jnp.mean(x * x, axis=-1, keepdims=True)
    o_ref[...] = x * lax.rsqrt(ms + RMS_EPS) * g_ref[...]


def _rmsnorm(x, g, tm):
    n, d = x.shape
    return pl.pallas_call(
        _rmsnorm_body,
        grid=(n // tm,),
        in_specs=[pl.BlockSpec((tm, d), lambda i: (i, 0)), pl.BlockSpec((1, d), lambda i: (0, 0))],
        out_specs=pl.BlockSpec((tm, d), lambda i: (i, 0)),
        out_shape=jax.ShapeDtypeStruct((n, d), F32),
        compiler_params=_params("parallel"),
        name="rmsnorm",
    )(x, g.reshape(1, d))


def _mm_body(x_ref, w_ref, o_ref):
    o_ref[...] = _dot(x_ref[...].astype(BF16), w_ref[...])


def _mm_res_body(x_ref, w_ref, r_ref, o_ref):
    o_ref[...] = r_ref[...] + _dot(x_ref[...].astype(BF16), w_ref[...])


def _matmul(x, w, tm, residual=None):
    n, k = x.shape
    m = w.shape[1]
    row = lambda i: (i, 0)
    in_specs = [pl.BlockSpec((tm, k), row), pl.BlockSpec((k, m), lambda i: (0, 0))]
    args = [x, w]
    body = _mm_body
    if residual is not None:
        in_specs.append(pl.BlockSpec((tm, m), row))
        args.append(residual)
        body = _mm_res_body
    return pl.pallas_call(
        body,
        grid=(n // tm,),
        in_specs=in_specs,
        out_specs=pl.BlockSpec((tm, m), row),
        out_shape=jax.ShapeDtypeStruct((n, m), F32),
        compiler_params=_params("parallel"),
        name="matmul",
    )(*args)


def _token_shift(h_ref, tail_ref, prev_ref, i, tm, seq_len):
    h = h_ref[...]
    if prev_ref is not None:
        return h, prev_ref[...]
    rolled = pltpu.roll(h, 1, 0)
    starts_sequence = (i * tm) % seq_len == 0
    first = jnp.where(starts_sequence, 0.0, tail_ref[SUBLANES - 1:SUBLANES, :])
    row = lax.broadcasted_iota(jnp.int32, h.shape, 0)
    return h, jnp.where(row == 0, first, rolled)


def _shift_specs(tm, d, has_prev, grid_rank):
    if grid_rank == 1:
        cur = lambda i: (i, 0)
        tail = lambda i: (jnp.maximum(i * (tm // SUBLANES) - 1, 0), 0)
    else:
        cur = lambda p, i: (i, 0)
        tail = lambda p, i: (jnp.maximum(i * (tm // SUBLANES) - 1, 0), 0)
    if has_prev:
        return [pl.BlockSpec((tm, d), cur), pl.BlockSpec((tm, d), cur)]
    return [pl.BlockSpec((tm, d), cur), pl.BlockSpec((SUBLANES, d), tail)]


def _rkv_body(h_ref, aux_ref, mu_ref, w_ref, o_ref, *, tm, seq_len, has_prev):
    i = pl.program_id(1)
    h, prev = _token_shift(h_ref, None if has_prev else aux_ref, aux_ref if has_prev else None, i, tm, seq_len)
    x = h + (prev - h) * mu_ref[0]
    o_ref[0] = _dot(x.astype(BF16), w_ref[0])


def _rwkv_rkv(h, prev, mu3, w3, tm, seq_len):
    n, d = h.shape
    has_prev = prev is not None
    return pl.pallas_call(
        functools.partial(_rkv_body, tm=tm, seq_len=seq_len, has_prev=has_prev),
        grid=(3, n // tm),
        in_specs=_shift_specs(tm, d, has_prev, 2) + [
            pl.BlockSpec((1, 1, d), lambda p, i: (p, 0, 0)),
            pl.BlockSpec((1, d, d), lambda p, i: (p, 0, 0)),
        ],
        out_specs=pl.BlockSpec((1, tm, d), lambda p, i: (p, i, 0)),
        out_shape=jax.ShapeDtypeStruct((3, n, d), F32),
        compiler_params=_params("arbitrary", "arbitrary"),
        name="rwkv_rkv",
    )(h, prev if has_prev else h, mu3, w3)


def _lora_body(h_ref, aux_ref, mu_ref, w0_ref, a0_ref, w1_ref, w2_ref, a1_ref, a2_ref, g1_ref, g2_ref,
               decay_ref, a_ref, g_ref, *, tm, seq_len, has_prev):
    i = pl.program_id(0)
    h, prev = _token_shift(h_ref, None if has_prev else aux_ref, aux_ref if has_prev else None, i, tm, seq_len)
    xx = prev - h
    xw = (h + xx * mu_ref[0]).astype(BF16)
    xa = (h + xx * mu_ref[1]).astype(BF16)
    xg = (h + xx * mu_ref[2]).astype(BF16)
    lw = _dot(jnp.tanh(_dot(xw, w1_ref[...])).astype(BF16), w2_ref[...])
    w = -_softplus(-(w0_ref[...] + lw)) - 0.5
    decay_ref[...] = -jnp.exp(w)
    la = _dot(_dot(xa, a1_ref[...]).astype(BF16), a2_ref[...])
    a_ref[...] = _sigmoid(a0_ref[...] + la)
    g_ref[...] = _dot(_sigmoid(_dot(xg, g1_ref[...])).astype(BF16), g2_ref[...])


def _rwkv_lora(h, prev, mu3, w0, a0, w1, w2, a1, a2, g1, g2, tm, seq_len):
    n, d = h.shape
    has_prev = prev is not None
    full = lambda arr: pl.BlockSpec(arr.shape, lambda i, nd=arr.ndim: (0,) * nd)
    row = pl.BlockSpec((tm, d), lambda i: (i, 0))
    out = jax.ShapeDtypeStruct((n, d), F32)
    return pl.pallas_call(
        functools.partial(_lora_body, tm=tm, seq_len=seq_len, has_prev=has_prev),
        grid=(n // tm,),
        in_specs=_shift_specs(tm, d, has_prev, 1) + [full(mu3), full(w0), full(a0), full(w1), full(w2),
                                                     full(a1), full(a2), full(g1), full(g2)],
        out_specs=[row, row, row],
        out_shape=[out, out, out],
        compiler_params=_params("parallel"),
        name="rwkv_lora",
    )(h, prev if has_prev else h, mu3, w0, a0, w1, w2, a1, a2, g1, g2)


STEP_SEQS = SUBLANES


def _segsum(x, ones_bd):
    return _dot_hilo(x, ones_bd)


def _wkv_step_body(r_ref, k_ref, v_ref, ld_ref, a_ref, g_ref, kkw_ref, kaw_ref, rkw_ref, lnw_ref, lnb_ref, s0_ref,
                   o_ref, st_ref):
    ones_bd = _block_ones(LANES, RWKV_HEAD)
    rows = lax.broadcasted_iota(jnp.int32, (RWKV_HEAD, LANES), 0)
    lanes = lax.broadcasted_iota(jnp.int32, (RWKV_HEAD, LANES), 1)
    diag = (lanes % RWKV_HEAD) == rows
    pairs = range(HEAD_PAIRS)
    lanes_of = [slice(p * LANES, (p + 1) * LANES) for p in pairs]
    r = [r_ref[:, sl] for sl in lanes_of]
    k = [k_ref[:, sl] for sl in lanes_of]
    v = [v_ref[:, sl] for sl in lanes_of]
    a = [a_ref[:, sl] for sl in lanes_of]
    d = [jnp.exp(ld_ref[:, sl]) for sl in lanes_of]
    kk_raw = [k[p] * kkw_ref[:, lanes_of[p]] for p in pairs]
    kk_norm = [jnp.sqrt(_segsum(x * x, ones_bd)) for x in kk_raw]
    kk = [kk_raw[p] / jnp.maximum(kk_norm[p], 1e-12) for p in pairs]
    kp = [k[p] * (1.0 + (a[p] - 1.0) * kaw_ref[:, lanes_of[p]]) for p in pairs]
    ka = [kk[p] * a[p] for p in pairs]
    dr = [d[p] * r[p] for p in pairs]
    c1 = [_segsum(ka[p] * r[p], ones_bd) for p in pairs]
    c2 = [_segsum(kp[p] * r[p], ones_bd) for p in pairs]
    y_rows = [[] for _ in pairs]
    for s in range(STEP_SEQS):
        row = lambda x: x[s:s + 1, :]
        state = [jnp.concatenate([s0_ref[s, 2 * p], s0_ref[s, 2 * p + 1]], axis=1) for p in pairs]
        s_kk = [_dot_hilo(state[p] * row(kk[p]), ones_bd) for p in pairs]
        y_part = [_dot_hilo(state[p] * row(dr[p]), ones_bd) for p in pairs]
        v_col = [_dot_hilo(jnp.where(diag, row(v[p]), 0.0), ones_bd) for p in pairs]
        for p in pairs:
            new = state[p] * row(d[p]) - s_kk[p] * row(ka[p]) + v_col[p] * row(kp[p])
            st_ref[s, 2 * p] = new[:, :RWKV_HEAD]
            st_ref[s, 2 * p + 1] = new[:, RWKV_HEAD:]
            y_col = y_part[p] - s_kk[p] * row(c1[p]) + v_col[p] * row(c2[p])
            y_rows[p].append(jnp.sum(jnp.where(diag, y_col, 0.0), axis=0, keepdims=True))
    y = [jnp.concatenate(y_rows[p], axis=0) for p in pairs]
    mean = [_segsum(x, ones_bd) * (1.0 / RWKV_HEAD) for x in y]
    yc = [y[p] - mean[p] for p in pairs]
    var = [_segsum(x * x, ones_bd) * (1.0 / RWKV_HEAD) for x in yc]
    bonus_dot = [_segsum(r[p] * kp[p] * rkw_ref[:, lanes_of[p]], ones_bd) for p in pairs]
    for p in pairs:
        sl = lanes_of[p]
        yn = yc[p] * lax.rsqrt(var[p] + GN_EPS) * lnw_ref[:, sl] + lnb_ref[:, sl]
        o_ref[:, sl] = (yn + bonus_dot[p] * v[p]) * g_ref[:, sl]


def _wkv_step(rkv, logd, a, g, kkw, kaw, rkw, lnw, lnb, s0):
    batch, d = logd.shape
    n = STEP_SEQS
    tok = lambda which: pl.BlockSpec((None, n, d), lambda i, which=which: (which, i, 0))
    tok2 = pl.BlockSpec((n, d), lambda i: (i, 0))
    vec = pl.BlockSpec((1, d), lambda i: (0, 0))
    state = pl.BlockSpec((n, RWKV_HEADS, RWKV_HEAD, RWKV_HEAD), lambda i: (i, 0, 0, 0))
    return pl.pallas_call(
        _wkv_step_body,
        grid=(batch // n,),
        in_specs=[tok(0), tok(1), tok(2), tok2, tok2, tok2, vec, vec, vec, vec, vec, state],
        out_specs=[tok2, state],
        out_shape=[jax.ShapeDtypeStruct((batch, d), F32),
                   jax.ShapeDtypeStruct((batch, RWKV_HEADS, RWKV_HEAD, RWKV_HEAD), F32)],
        compiler_params=_params("parallel"),
        name="wkv_step",
    )(rkv, rkv, rkv, logd, a, g, kkw, kaw, rkw, lnw, lnb, s0)


WKV_CHUNK = 64
WKV_STEP_PAIRS = 8
WKV_STEP_LANES = WKV_STEP_PAIRS * LANES


def _dot_nt(a, b):
    return lax.dot_general(a, b, (((1,), (1,)), ((), ())), preferred_element_type=F32)


def _split3(x):
    hi = x.astype(BF16)
    r1 = x - hi.astype(F32)
    mid = r1.astype(BF16)
    lo = (r1 - mid.astype(F32)).astype(BF16)
    return hi, mid, lo


def _wkv_chunk_body(r_ref, k_ref, v_ref, ld_ref, a_ref, g_ref, kkw_ref, kaw_ref, rkw_ref, lnw_ref, lnb_ref,
                    o_ref, st_ref, s_scr):
    t_chunk = pl.program_id(2)
    c = WKV_CHUNK
    ones_bd = _block_ones(LANES, RWKV_HEAD)
    row = lax.broadcasted_iota(jnp.int32, (LANES, LANES), 0)
    col = lax.broadcasted_iota(jnp.int32, (LANES, LANES), 1)
    strict = col < row
    incl = col <= row
    eye = jnp.where(row == col, 1.0, 0.0)
    ltri = jnp.where(incl, 1.0, 0.0).astype(BF16)
    head0 = lax.broadcasted_iota(jnp.int32, (c, LANES), 1) < RWKV_HEAD
    stack = lambda x: jnp.concatenate([jnp.where(head0, x, 0.0), jnp.where(head0, 0.0, x)], axis=0)

    @pl.when(t_chunk == 0)
    def _zero_state():
        s_scr[...] = jnp.zeros_like(s_scr)

    pairs = range(WKV_STEP_PAIRS)
    lanes_of = [slice(pp * LANES, (pp + 1) * LANES) for pp in pairs]
    each = lambda fn, *lists: [fn(*args) for args in zip(*lists)]
    bf = lambda x: x.astype(BF16)
    r = [r_ref[0, :, sl] for sl in lanes_of]
    k = [k_ref[0, :, sl] for sl in lanes_of]
    v = [v_ref[0, :, sl] for sl in lanes_of]
    ld = [ld_ref[0, :, sl] for sl in lanes_of]
    a = [a_ref[0, :, sl] for sl in lanes_of]
    kk_raw = each(lambda x, sl: x * kkw_ref[:, sl], k, lanes_of)
    kk_norm = each(lambda x: jnp.sqrt(_segsum(x * x, ones_bd)), kk_raw)
    kk = each(lambda x, n: x / jnp.maximum(n, 1e-12), kk_raw, kk_norm)
    kp = each(lambda x, y, sl: x * (1.0 + (y - 1.0) * kaw_ref[:, sl]), k, a, lanes_of)
    ka = each(lambda x, y: x * y, kk, a)
    ld_pieces = each(lambda x: _split3(jnp.concatenate([x, jnp.zeros_like(x)], axis=0)), ld)
    cum2 = [_dot(ltri, p3[0]) + _dot(ltri, p3[1]) + _dot(ltri, p3[2]) for p3 in ld_pieces]
    cum = [x[:c] for x in cum2]
    gam_end = [jnp.exp(x[c:]) for x in cum2]
    gam_inv = [jnp.exp(-x) for x in cum]
    at = each(lambda x, y: x * y, ka, gam_inv)
    kt = each(lambda x, y: x * y, kp, gam_inv)
    x_st = each(lambda kk_, r_, cum_, ld_: bf(jnp.concatenate(
        [stack(kk_ * jnp.exp(cum_ - ld_)), stack(r_ * jnp.exp(cum_))], axis=0)), kk, r, cum, ld)
    y_st = each(lambda x, y: bf(jnp.concatenate([stack(x), stack(y)], axis=0)), at, kt)
    gram = each(_dot_nt, x_st, y_st)
    m_a = [jnp.where(strict, x[:LANES, :LANES], 0.0) for x in gram]
    m_k = [bf(jnp.where(strict, x[:LANES, LANES:], 0.0)) for x in gram]
    n_a = [bf(jnp.where(incl, x[LANES:, :LANES], 0.0)) for x in gram]
    n_k = [bf(jnp.where(incl, x[LANES:, LANES:], 0.0)) for x in gram]
    power = [-x for x in m_a]
    inv = [eye + x for x in power]
    for _ in range(5):
        power = [_dot(bf(x), bf(x)) for x in power]
        inv = each(lambda t, p: t + _dot(bf(t), bf(p)), inv, power)
    inv = [bf(x) for x in inv]
    v_st = [stack(x) for x in v]
    v_stb = [bf(x) for x in v_st]
    v_stt = [bf(x.T) for x in v_st]
    at_end = each(lambda x, y: bf(stack(x * y)), at, gam_end)
    kt_end = each(lambda x, y: bf(stack(x * y)), kt, gam_end)
    s_from_v = each(_dot, v_stt, kt_end)
    m_kv = each(_dot, m_k, v_stb)
    n_kv = each(_dot, n_k, v_stb)
    s = [s_scr[pp] for pp in pairs]
    proj = each(lambda x, s_: _dot_nt(x, bf(s_)), x_st, s)
    u = each(lambda t, p, mv: -_dot(t, bf(p[:LANES] + mv)), inv, proj, m_kv)
    y2 = each(lambda p, n, u_, nv: p[LANES:] + _dot(n, bf(u_)) + nv, proj, n_a, u, n_kv)
    s_from_u = each(lambda u_, x: _dot(bf(u_.T), x), u, at_end)
    for pp in pairs:
        s_scr[pp] = s[pp] * jnp.concatenate([gam_end[pp], gam_end[pp]], axis=0) + s_from_u[pp] + s_from_v[pp]
    y = [x[:c] + x[c:] for x in y2]
    mean = [_segsum(x, ones_bd) * (1.0 / RWKV_HEAD) for x in y]
    yc = each(lambda x, m: x - m, y, mean)
    var = [_segsum(x * x, ones_bd) * (1.0 / RWKV_HEAD) for x in yc]
    bonus_dot = each(lambda r_, kp_, sl: _segsum(r_ * kp_ * rkw_ref[:, sl], ones_bd), r, kp, lanes_of)
    for pp in pairs:
        sl = lanes_of[pp]
        yn = yc[pp] * lax.rsqrt(var[pp] + GN_EPS) * lnw_ref[:, sl] + lnb_ref[:, sl]
        o_ref[0, :, sl] = (yn + bonus_dot[pp] * v[pp]) * g_ref[0, :, sl]

    @pl.when(t_chunk == pl.num_programs(2) - 1)
    def _store_state():
        for pp in range(WKV_STEP_PAIRS):
            s = s_scr[pp]
            st_ref[0, 2 * pp] = s[:RWKV_HEAD, :RWKV_HEAD]
            st_ref[0, 2 * pp + 1] = s[RWKV_HEAD:, RWKV_HEAD:]


def _wkv_chunked(rkv, logd, a, g, kkw, kaw, rkw, lnw, lnb, batch, seq_len):
    d = D_MODEL
    c = WKV_CHUNK
    w = WKV_STEP_LANES
    rkv4 = rkv.reshape(3, batch, seq_len, d)
    seq3 = lambda x: x.reshape(batch, seq_len, d)
    tok = lambda which: pl.BlockSpec((None, 1, c, w), lambda b, p, t, which=which: (which, b, t, p))
    tok3 = pl.BlockSpec((1, c, w), lambda b, p, t: (b, t, p))
    vec = pl.BlockSpec((1, w), lambda b, p, t: (0, p))
    state = pl.BlockSpec((1, 2 * WKV_STEP_PAIRS, RWKV_HEAD, RWKV_HEAD), lambda b, p, t: (b, p, 0, 0))
    o, st = pl.pallas_call(
        _wkv_chunk_body,
        grid=(batch, d // w, seq_len // c),
        in_specs=[tok(0), tok(1), tok(2), tok3, tok3, tok3, vec, vec, vec, vec, vec],
        out_specs=[tok3, state],
        out_shape=[jax.ShapeDtypeStruct((batch, seq_len, d), F32),
                   jax.ShapeDtypeStruct((batch, RWKV_HEADS, RWKV_HEAD, RWKV_HEAD), F32)],
        scratch_shapes=[pltpu.VMEM((WKV_STEP_PAIRS, LANES, LANES), F32)],
        compiler_params=_params("parallel", "parallel", "arbitrary"),
        name="wkv_chunked",
    )(rkv4, rkv4, rkv4, seq3(logd), seq3(a), seq3(g), kkw, kaw, rkw, lnw, lnb)
    return o.reshape(batch * seq_len, d), st


SB_STEP_HEADS = 8
SB_STEP_LANES = SB_STEP_HEADS * SB_HEAD_DIM


def _sb_prompt_body(q_ref, k_ref, v_ref, b_ref, o_ref, acc_scr, tail_scr):
    qi = pl.program_id(2)
    scale = SB_HEAD_DIM ** -0.5
    t_idx = lax.broadcasted_iota(jnp.int32, (Q_BLOCK, Q_BLOCK), 0)
    s_idx = lax.broadcasted_iota(jnp.int32, (Q_BLOCK, Q_BLOCK), 1)
    later = jnp.where(t_idx > s_idx, 1.0, 0.0).astype(BF16)
    later_and_ones = jnp.concatenate([later, jnp.ones((Q_BLOCK, Q_BLOCK), BF16)], axis=1)

    acc_scr[...] = jnp.zeros_like(acc_scr)
    tail_scr[...] = jnp.zeros_like(tail_scr)

    def body(j, carry):
        kb = qi - j
        start = pl.multiple_of(kb * Q_BLOCK, Q_BLOCK)
        mask = (kb * Q_BLOCK + s_idx) < (qi * Q_BLOCK + t_idx)
        heads = range(SB_STEP_HEADS)
        lanes_of = [slice(hh * SB_HEAD_DIM, (hh + 1) * SB_HEAD_DIM) for hh in heads]
        z = [_dot_nt(q_ref[0, :, sl].astype(BF16), k_ref[0, pl.ds(start, Q_BLOCK), sl].astype(BF16)) * scale
             + b_ref[:, sl] for sl in lanes_of]
        sp = [_softplus(x) for x in z]
        log_not = [jnp.where(mask, -x, 0.0) for x in sp]
        hi = [x.astype(BF16) for x in log_not]
        lo = [(x - h.astype(F32)).astype(BF16) for x, h in zip(log_not, hi)]
        sums_hi = [_dot(x, later_and_ones) for x in hi]
        sums_lo = [_dot(x, later_and_ones) for x in lo]
        sums = [x + y for x, y in zip(sums_hi, sums_lo)]
        tail = [tail_scr[hh] for hh in heads]
        p = [jnp.where(mask, jnp.exp(z[hh] - sp[hh] + sums[hh][:, :Q_BLOCK] + tail[hh]), 0.0).astype(BF16)
             for hh in heads]
        pv = [_dot(p[hh], v_ref[0, pl.ds(start, Q_BLOCK), lanes_of[hh]].astype(BF16)) for hh in heads]
        for hh in heads:
            acc_scr[hh] += pv[hh]
            tail_scr[hh] = tail[hh] + sums[hh][:, Q_BLOCK:]
        return carry

    lax.fori_loop(0, qi + 1, body, 0)
    for hh in range(SB_STEP_HEADS):
        o_ref[0, :, hh * SB_HEAD_DIM:(hh + 1) * SB_HEAD_DIM] = acc_scr[hh]


def _sb_prompt(q, k, v, bias_cols, batch, seq_len):
    d = D_MODEL
    w = SB_STEP_LANES
    seq3 = lambda x: x.reshape(batch, seq_len, d)
    qspec = pl.BlockSpec((1, Q_BLOCK, w), lambda b, h, i: (b, i, h))
    kspec = pl.BlockSpec((1, seq_len, w), lambda b, h, i: (b, 0, h))
    vspec = kspec
    block_state = pltpu.VMEM((SB_STEP_HEADS, Q_BLOCK, SB_HEAD_DIM), F32)
    out = pl.pallas_call(
        _sb_prompt_body,
        grid=(batch, d // w, seq_len // Q_BLOCK),
        in_specs=[qspec, kspec, vspec, pl.BlockSpec((1, w), lambda b, h, i: (0, h))],
        out_specs=pl.BlockSpec((1, Q_BLOCK, w), lambda b, h, i: (b, i, h)),
        out_shape=jax.ShapeDtypeStruct((batch, seq_len, d), F32),
        scratch_shapes=[block_state, block_state],
        compiler_params=_params("parallel", "parallel", "arbitrary"),
        name="sb_prompt",
    )(seq3(q), seq3(k), seq3(v), bias_cols)
    return out.reshape(batch * seq_len, d)


PAGE_ROWS = PAGE_SIZE * SB_HEADS


SB_STEP_SEQS = 2


def _sb_sample_body(pt_ref, q_ref, kn_ref, vn_ref, *refs):
    n = SB_STEP_SEQS
    kc_refs, vc_refs = refs[:n], refs[n:2 * n]
    b_ref, col_ref, exp_ref, o_ref, acc_scr, tail_scr = refs[2 * n:]
    seqs = range(n)
    step = pl.program_id(1)
    n_steps = pl.num_programs(1)
    scale = SB_HEAD_DIM ** -0.5
    bias = b_ref[...]
    head = lax.broadcasted_iota(jnp.int32, (SB_HEADS, PAGE_ROWS), 0)
    own_head = lax.broadcasted_iota(jnp.int32, (SB_HEADS, PAGE_ROWS), 1) % SB_HEADS == head
    r_idx = lax.broadcasted_iota(jnp.int32, (PAGE_SIZE, PAGE_SIZE), 0)
    c_idx = lax.broadcasted_iota(jnp.int32, (PAGE_SIZE, PAGE_SIZE), 1)
    later = jnp.where(r_idx > c_idx, 1.0, 0.0).astype(BF16)
    ones = jnp.ones((PAGE_SIZE, PAGE_SIZE), BF16)

    @pl.when(step == 0)
    def _init():
        n_past = n_steps * PAGE_SIZE
        new_is_before = jnp.full((SB_HEADS, LANES), n_past, jnp.int32) < n_past
        for i in seqs:
            z_new = jnp.sum(q_ref[i] * kn_ref[i], axis=-1, keepdims=True) * scale + bias
            sp_new = _softplus(z_new)
            tail_scr[i] = jnp.where(new_is_before, -sp_new, 0.0)
            acc_scr[i] = jnp.where(new_is_before, jnp.exp(z_new - sp_new), 0.0) * vn_ref[i]

    z_all = [_dot_nt(q_ref[i].astype(BF16), kc_refs[i][0].astype(BF16)) for i in seqs]
    z_own = [_split3(jnp.where(own_head, x, 0.0)) for x in z_all]
    collapse = col_ref[...]
    z = [(_dot(p3[0], collapse) + _dot(p3[1], collapse) + _dot(p3[2], collapse)) * scale + bias for p3 in z_own]
    sp = [_softplus(x) for x in z]
    hi = [(-x).astype(BF16) for x in sp]
    lo = [(-x - h.astype(F32)).astype(BF16) for x, h in zip(sp, hi)]
    after = [_dot(h, later) + _dot(l, later) for h, l in zip(hi, lo)]
    total = [_dot(h, ones) + _dot(l, ones) for h, l in zip(hi, lo)]
    tail = [tail_scr[i] for i in seqs]
    p = [jnp.exp(z[i] - sp[i] + after[i] + tail[i]).astype(BF16) for i in seqs]
    p_rows = [jnp.where(own_head, _dot(x, exp_ref[...]), 0.0).astype(BF16) for x in p]
    pv = [_dot(p_rows[i], vc_refs[i][0].astype(BF16)) for i in seqs]
    for i in seqs:
        acc_scr[i] += pv[i]
        tail_scr[i] = tail[i] + total[i]

    @pl.when(step == n_steps - 1)
    def _finish():
        o_ref[...] = acc_scr[...]


def _sb_sample(q, k_new, v_new, cache_k, cache_v, page_table, bias):
    batch, d = q.shape
    n_pages = page_table.shape[1]
    n_phys = cache_k.shape[0]
    pages = lambda c: c.reshape(n_phys, PAGE_ROWS, SB_HEAD_DIM)
    row_token = lax.broadcasted_iota(jnp.int32, (PAGE_ROWS, PAGE_SIZE), 0) // SB_HEADS
    collapse = (row_token == lax.broadcasted_iota(jnp.int32, (PAGE_ROWS, PAGE_SIZE), 1)).astype(BF16)
    bias_lanes = jnp.broadcast_to(bias[:, None], (SB_HEADS, LANES)).astype(F32)
    n = SB_STEP_SEQS
    tok = pl.BlockSpec((n, SB_HEADS, SB_HEAD_DIM), lambda b, s, pt: (b, 0, 0))
    page = lambda i: pl.BlockSpec((1, PAGE_ROWS, SB_HEAD_DIM),
                                  lambda b, s, pt, i=i: (pt[b * n + i, n_pages - 1 - s], 0, 0))
    const = lambda shape: pl.BlockSpec(shape, lambda b, s, pt: (0, 0))
    heads3 = lambda x: x.reshape(batch, SB_HEADS, SB_HEAD_DIM)
    per_seq = pltpu.VMEM((n, SB_HEADS, SB_HEAD_DIM), F32)
    out = pl.pallas_call(
        _sb_sample_body,
        grid_spec=pltpu.PrefetchScalarGridSpec(
            num_scalar_prefetch=1,
            grid=(batch // n, n_pages),
            in_specs=[tok, tok, tok] + [page(i) for i in range(n)] * 2 + [
                const((SB_HEADS, LANES)), const((PAGE_ROWS, PAGE_SIZE)), const((PAGE_SIZE, PAGE_ROWS))],
            out_specs=tok,
            scratch_shapes=[per_seq, per_seq],
        ),
        out_shape=jax.ShapeDtypeStruct((batch, SB_HEADS, SB_HEAD_DIM), F32),
        compiler_params=_params("parallel", "arbitrary"),
        name="sb_sample",
    )(page_table, heads3(q), heads3(k_new), heads3(v_new), *([pages(cache_k)] * n), *([pages(cache_v)] * n),
      bias_lanes, collapse, collapse.T)
    return out.reshape(batch, d)


ROUTER_OFFSET = N_GROUPS


def _router_body(x_ref, w_ref, b_ref, gate_ref, route_ref, count_ref, count_scr):
    logits = jnp.dot(x_ref[...], w_ref[...], preferred_element_type=F32,
                     precision=lax.Precision.HIGHEST) + b_ref[...]
    lane = lax.broadcasted_iota(jnp.int32, logits.shape, 1)
    neg = -jnp.inf
    first_lane_of = lambda hit: jnp.min(jnp.where(hit, lane, LANES), axis=-1, keepdims=True)
    is_group = lane < N_GROUPS
    glog = jnp.where(is_group, logits, neg)
    gmax = jnp.max(glog, axis=-1, keepdims=True)
    g_sel = first_lane_of(glog == gmax)
    p_group = 1.0 / jnp.sum(jnp.exp(glog - gmax), axis=-1, keepdims=True)
    in_group = ((lane - ROUTER_OFFSET) // EXPERTS_PER_GROUP == g_sel) & (lane >= ROUTER_OFFSET) \
        & (lane < ROUTER_OFFSET + N_EXPERTS)
    elog = jnp.where(in_group, logits, neg)
    top1 = jnp.max(elog, axis=-1, keepdims=True)
    lane1 = first_lane_of(elog == top1)
    rest = jnp.where(lane == lane1, neg, elog)
    top2 = jnp.max(rest, axis=-1, keepdims=True)
    lane2 = first_lane_of(rest == top2)
    e2 = jnp.exp(top2 - top1)
    w1 = p_group / (1.0 + e2)
    w2 = p_group * e2 / (1.0 + e2)
    gate_ref[...] = jnp.where(lane == lane1, w1, 0.0) + jnp.where(lane == lane2, w2, 0.0)
    @pl.when(pl.program_id(0) == 0)
    def _zero_counts():
        count_scr[...] = jnp.zeros_like(count_scr)

    tm = logits.shape[0]
    earlier = (lax.broadcasted_iota(jnp.int32, (tm, tm), 1)
               < lax.broadcasted_iota(jnp.int32, (tm, tm), 0)).astype(BF16)
    hot1 = jnp.where(lane == lane1, 1.0, 0.0)
    hot2 = jnp.where(lane == lane2, 1.0, 0.0)
    seen = count_scr[0:1, :]
    total1 = jnp.sum(hot1, axis=0, keepdims=True)
    rank1 = jnp.sum(hot1 * (seen + _dot(earlier, hot1.astype(BF16))), axis=-1, keepdims=True)
    rank2 = jnp.sum(hot2 * (seen + total1 + _dot(earlier, hot2.astype(BF16))), axis=-1, keepdims=True)
    seen = seen + total1 + jnp.sum(hot2, axis=0, keepdims=True)
    count_scr[...] = jnp.broadcast_to(seen, count_scr.shape)
    count_ref[...] = jnp.broadcast_to(seen, count_ref.shape)
    route_ref[...] = (jnp.where(lane == 0, (lane1 - ROUTER_OFFSET).astype(F32), 0.0)
                      + jnp.where(lane == 1, (lane2 - ROUTER_OFFSET).astype(F32), 0.0)
                      + jnp.where(lane == 2, w1, 0.0) + jnp.where(lane == 3, w2, 0.0)
                      + jnp.where(lane == 4, rank1, 0.0) + jnp.where(lane == 5, rank2, 0.0))


def _router(x, w_router, b_router, tm):
    n, d = x.shape
    out = jax.ShapeDtypeStruct((n, LANES), F32)
    row = pl.BlockSpec((tm, LANES), lambda i: (i, 0))
    counts = pl.BlockSpec((SUBLANES, LANES), lambda i: (0, 0))
    return pl.pallas_call(
        _router_body,
        grid=(n // tm,),
        in_specs=[pl.BlockSpec((tm, d), lambda i: (i, 0)),
                  pl.BlockSpec((d, LANES), lambda i: (0, 0)),
                  pl.BlockSpec((1, LANES), lambda i: (0, 0))],
        out_specs=[row, row, counts],
        out_shape=[out, out, jax.ShapeDtypeStruct((SUBLANES, LANES), F32)],
        scratch_shapes=[pltpu.VMEM((SUBLANES, LANES), F32)],
        compiler_params=_params("arbitrary"),
        name="moe_router",
    )(x, w_router, b_router)


def _moe_body(x_ref, gate_ref, res_ref, wg_ref, wu_ref, wd_ref, o_ref, xb_scr):
    e = pl.program_id(1)

    @pl.when(e == 0)
    def _init():
        xb_scr[...] = x_ref[...].astype(BF16)
        o_ref[...] = res_ref[...]

    xb = xb_scr[...]
    hg = _dot(xb, wg_ref[0].astype(BF16))
    hu = _dot(xb, wu_ref[0].astype(BF16))
    gate = gate_ref[...]
    lane = lax.broadcasted_iota(jnp.int32, gate.shape, 1)
    ge = jnp.sum(jnp.where(lane == e + ROUTER_OFFSET, gate, 0.0), axis=-1, keepdims=True)
    act = (hg * _sigmoid(hg)) * hu * ge
    o_ref[...] += _dot(act.astype(BF16), wd_ref[0].astype(BF16))


def _moe_dense(x, gate, residual, wg, wu, wd, tm):
    n, d = x.shape
    row = lambda i, e: (i, 0)
    return pl.pallas_call(
        _moe_body,
        grid=(n // tm, N_EXPERTS),
        in_specs=[pl.BlockSpec((tm, d), row), pl.BlockSpec((tm, LANES), row), pl.BlockSpec((tm, d), row),
                  pl.BlockSpec((1, d, D_EXPERT), lambda i, e: (e, 0, 0)),
                  pl.BlockSpec((1, d, D_EXPERT), lambda i, e: (e, 0, 0)),
                  pl.BlockSpec((1, D_EXPERT, d), lambda i, e: (e, 0, 0))],
        out_specs=pl.BlockSpec((tm, d), row),
        out_shape=jax.ShapeDtypeStruct((n, d), F32),
        scratch_shapes=[pltpu.VMEM((tm, d), BF16)],
        compiler_params=_params("parallel", "arbitrary"),
        name="moe_experts",
    )(x, gate, residual, wg, wu, wd)


MOE_TILE = 256
COMBINE_TILE = 256


def _row_copy(src_hbm, index_ref, base, buf, sem):
    def copy(r):
        return pltpu.make_async_copy(src_hbm.at[pl.ds(index_ref[base + r], 1)], buf.at[pl.ds(r, 1)], sem)
    return copy


def _for_rows(n_rows, fn):
    def body(r, carry):
        fn(r)
        return carry
    lax.fori_loop(0, n_rows, body, 0, unroll=8)


def _gather_pipeline(step, n_steps, copies_of, n_rows):
    slot = step % 2

    @pl.when(step == 0)
    def _prime():
        _for_rows(n_rows, lambda r: copies_of(0, 0)(r).start())

    @pl.when(step + 1 < n_steps)
    def _prefetch():
        _for_rows(n_rows, lambda r: copies_of(step + 1, 1 - slot)(r).start())

    _for_rows(n_rows, lambda r: copies_of(step, slot)(r).wait())
    return slot


def _moe_dispatch_body(pos_ref, h_ref, zeros_hbm, xs_hbm, sem):
    del zeros_hbm
    i = pl.program_id(0)
    tm = h_ref.shape[0]

    def copy(j):
        r = j % tm
        return pltpu.make_async_copy(h_ref.at[pl.ds(r, 1)], xs_hbm.at[pl.ds(pos_ref[i * 2 * tm + j], 1)], sem)

    _for_rows(2 * tm, lambda j: copy(j).start())
    _for_rows(2 * tm, lambda j: copy(j).wait())


def _moe_sparse_body(tile_expert_ref, n_used_ref, x_ref, wg_ref, wu_ref, wd_ref, y_ref, wg_bf, wu_bf, wd_bf):
    t = pl.program_id(0)

    @pl.when(jnp.logical_or(t == 0, tile_expert_ref[t] != tile_expert_ref[jnp.maximum(t - 1, 0)]))
    def _new_expert():
        wg_bf[...] = wg_ref[0].astype(BF16)
        wu_bf[...] = wu_ref[0].astype(BF16)
        wd_bf[...] = wd_ref[0].astype(BF16)

    @pl.when(t < n_used_ref[0])
    def _compute():
        x = x_ref[...].astype(BF16)
        hg = _dot(x, wg_bf[...])
        hu = _dot(x, wu_bf[...])
        y_ref[...] = _dot(((hg * _sigmoid(hg)) * hu).astype(BF16), wd_bf[...])

    @pl.when(t >= n_used_ref[0])
    def _unused_tile():
        y_ref[...] = jnp.zeros_like(y_ref)


def _moe_combine_body(pos_ref, x_ref, route_ref, y_hbm, o_ref, ybuf, sem):
    i = pl.program_id(0)
    tm = x_ref.shape[0]
    copies_of = lambda tile, s: _row_copy(y_hbm, pos_ref, tile * 2 * tm, ybuf.at[s], sem.at[s])
    slot = _gather_pipeline(i, pl.num_programs(0), copies_of, 2 * tm)
    rows = ybuf[slot]
    route = route_ref[...]
    o_ref[...] = x_ref[...] + route[:, 2:3] * rows[:tm] + route[:, 3:4] * rows[tm:]


def _route_plan(route, counts, tile, tm):
    n = route.shape[0]
    counts = counts[0, ROUTER_OFFSET:ROUTER_OFFSET + N_EXPERTS].astype(jnp.int32)
    padded = (counts + tile - 1) // tile * tile
    ends = jnp.cumsum(padded)
    offs = (ends - padded).astype(F32)
    expert = route[:, 0:2]
    hot = expert[:, :, None] == jnp.arange(N_EXPERTS, dtype=F32)[None, None, :]
    pos = (jnp.sum(jnp.where(hot, offs[None, None, :], 0.0), axis=-1) + route[:, 4:6]).astype(jnp.int32)
    n_tiles = (2 * n + N_EXPERTS * tile) // tile
    tile_expert = jnp.minimum(jnp.searchsorted(ends, jnp.arange(n_tiles, dtype=jnp.int32) * tile, side="right"),
                              N_EXPERTS - 1).astype(jnp.int32)
    n_used = (ends[-1] // tile).astype(jnp.int32).reshape(1)
    pos_tiled = pos.reshape(n // tm, tm, 2).transpose(0, 2, 1).reshape(2 * n)
    return pos_tiled, tile_expert, n_used


def _moe_sparse(h, route, counts, residual, wg, wu, wd):
    n, d = h.shape
    tile, tm = MOE_TILE, COMBINE_TILE
    pos_tiled, tile_expert, n_used = _route_plan(route, counts, tile, tm)
    p_rows = 2 * n + N_EXPERTS * tile
    token_tile = pl.BlockSpec((tm, d), lambda i, pos: (i, 0))
    xs = pl.pallas_call(
        _moe_dispatch_body,
        grid_spec=pltpu.PrefetchScalarGridSpec(
            num_scalar_prefetch=1,
            grid=(n // tm,),
            in_specs=[token_tile, pl.BlockSpec(memory_space=pl.ANY)],
            out_specs=pl.BlockSpec(memory_space=pl.ANY),
            scratch_shapes=[pltpu.SemaphoreType.DMA(())],
        ),
        out_shape=jax.ShapeDtypeStruct((p_rows, d), F32),
        input_output_aliases={2: 0},
        compiler_params=_params("arbitrary"),
        name="moe_dispatch",
    )(pos_tiled, h, jnp.zeros((p_rows, d), F32))
    y = pl.pallas_call(
        _moe_sparse_body,
        grid_spec=pltpu.PrefetchScalarGridSpec(
            num_scalar_prefetch=2,
            grid=(p_rows // tile,),
            in_specs=[pl.BlockSpec((tile, d), lambda t, te, nu: (t, 0)),
                      pl.BlockSpec((1, d, D_EXPERT), lambda t, te, nu: (te[t], 0, 0)),
                      pl.BlockSpec((1, d, D_EXPERT), lambda t, te, nu: (te[t], 0, 0)),
                      pl.BlockSpec((1, D_EXPERT, d), lambda t, te, nu: (te[t], 0, 0))],
            out_specs=pl.BlockSpec((tile, d), lambda t, te, nu: (t, 0)),
            scratch_shapes=[pltpu.VMEM((d, D_EXPERT), BF16), pltpu.VMEM((d, D_EXPERT), BF16),
                            pltpu.VMEM((D_EXPERT, d), BF16)],
        ),
        out_shape=jax.ShapeDtypeStruct((p_rows, d), F32),
        compiler_params=_params("arbitrary"),
        name="moe_sparse_experts",
    )(tile_expert, n_used, xs, wg, wu, wd)
    return pl.pallas_call(
        _moe_combine_body,
        grid_spec=pltpu.PrefetchScalarGridSpec(
            num_scalar_prefetch=1,
            grid=(n // tm,),
            in_specs=[token_tile, pl.BlockSpec((tm, LANES), lambda i, pos: (i, 0)),
                      pl.BlockSpec(memory_space=pl.ANY)],
            out_specs=token_tile,
            scratch_shapes=[pltpu.VMEM((2, 2 * tm, d), F32), pltpu.SemaphoreType.DMA((2,))],
        ),
        out_shape=jax.ShapeDtypeStruct((n, d), F32),
        compiler_params=_params("arbitrary"),
        name="moe_combine",
    )(pos_tiled, residual, route, y)


def _row_tile(n):
    return 256 if n % 256 == 0 else LANES


def _ffn(x, norm_g, w_router, b_router, wg, wu, wd):
    n = x.shape[0]
    tm = _row_tile(n)
    h = _rmsnorm(x, norm_g, tm)
    gate, route, counts = _router(h, w_router, b_router, tm)
    if n % COMBINE_TILE == 0 and n >= N_EXPERTS * MOE_TILE:
        return _moe_sparse(h, route, counts, x, wg, wu, wd)
    return _moe_dense(h, gate, x, wg, wu, wd, LANES)


def kernel(x_prompt, x_sample, state_tshift, state_wkv, cache_k, cache_v, page_table, norm_mix, norm_ffn, norm_final, rwkv_mu, rwkv_w0, rwkv_w1, rwkv_w2, rwkv_a0, rwkv_a1, rwkv_a2, rwkv_g1, rwkv_g2, rwkv_kk, rwkv_ka, rwkv_rk, rwkv_wr, rwkv_wk, rwkv_wv, rwkv_wo, rwkv_lnw, rwkv_lnb, sb_wqkv, sb_wo, sb_bias, moe_wgroup, moe_bgroup, moe_wexpert, moe_bexpert, moe_wgate, moe_wup, moe_wdown):
    d = D_MODEL
    bp, tp, _ = x_prompt.shape
    bs, ts, _ = x_sample.shape
    xp = x_prompt.reshape(bp * tp, d)
    xs = x_sample.reshape(bs * ts, d)
    row1 = lambda v: v.reshape(1, d)
    pad_cols = lambda w: jnp.pad(w, ((0, 0), (0, LORA_PAD - w.shape[1]))).astype(BF16)
    pad_rows = lambda w: jnp.pad(w, ((0, LORA_PAD - w.shape[0]), (0, 0))).astype(BF16)

    mu = rwkv_mu[0]
    mu_rkv = jnp.stack([mu[0], mu[2], mu[3]]).reshape(3, 1, d)
    mu_lora = jnp.stack([mu[1], mu[4], mu[5]]).reshape(3, 1, d)
    w_rkv = jnp.stack([rwkv_wr[0], rwkv_wk[0], rwkv_wv[0]]).astype(BF16)
    lora = (row1(rwkv_w0[0]), row1(rwkv_a0[0]), pad_cols(rwkv_w1[0]), pad_rows(rwkv_w2[0]),
            pad_cols(rwkv_a1[0]), pad_rows(rwkv_a2[0]), rwkv_g1[0].astype(BF16), rwkv_g2[0].astype(BF16))
    head_vecs = (row1(rwkv_kk[0]), row1(rwkv_ka[0]), row1(rwkv_rk[0]), row1(rwkv_lnw[0]), row1(rwkv_lnb[0]))
    wo = rwkv_wo[0].astype(BF16)

    def rwkv_layer(x, prev, s0, batch, seq_len):
        tm = _row_tile(x.shape[0])
        h = _rmsnorm(x, norm_mix[0], tm)
        rkv = _rwkv_rkv(h, prev, mu_rkv, w_rkv, tm, seq_len)
        logd, a, g = _rwkv_lora(h, prev, mu_lora, *lora, tm, seq_len)
        if s0 is None:
            o, s_final = _wkv_chunked(rkv, logd, a, g, *head_vecs, batch, seq_len)
        else:
            assert seq_len == 1, "a given initial state is only supported for single-token sequences"
            o, s_final = _wkv_step(rkv, logd, a, g, *head_vecs, s0)
        return _matmul(o, wo, tm, residual=x), h, s_final

    xp, hp, wkv_p = rwkv_layer(xp, None, None, bp, tp)
    xs, hs, wkv_s = rwkv_layer(xs, state_tshift[0], state_wkv[0], bs, ts)
    shift_p = hp.reshape(bp, tp, d)[:, -1]
    shift_s = hs.reshape(bs, ts, d)[:, -1]

    def ffn(x, i):
        w_router = jnp.pad(jnp.concatenate([moe_wgroup[i], moe_wexpert[i]], axis=1),
                           ((0, 0), (0, LANES - N_GROUPS - N_EXPERTS)))
        b_router = jnp.pad(jnp.concatenate([moe_bgroup[i], moe_bexpert[i]]),
                           (0, LANES - N_GROUPS - N_EXPERTS)).reshape(1, LANES)
        return _ffn(x, norm_ffn[i], w_router, b_router, moe_wgate[i], moe_wup[i], moe_wdown[i])

    xp = ffn(xp, 0)
    xs = ffn(xs, 0)

    w_qkv = jnp.stack(jnp.split(sb_wqkv[0], 3, axis=-1)).astype(BF16)
    w_att_out = sb_wo[0].astype(BF16)
    bias_cols = jnp.repeat(sb_bias[0].astype(F32), SB_HEAD_DIM).reshape(1, d)

    def qkv_proj(x):
        tm = _row_tile(x.shape[0])
        h = _rmsnorm(x, norm_mix[1], tm)
        return [_matmul(h, w_qkv[j], tm) for j in range(3)], tm

    qkv_p, tm_p = qkv_proj(xp)
    att_p = _sb_prompt(*qkv_p, bias_cols, bp, tp)
    xp = _matmul(att_p, w_att_out, tm_p, residual=xp)

    qkv_s, tm_s = qkv_proj(xs)
    att_s = _sb_sample(qkv_s[0], qkv_s[1], qkv_s[2], cache_k[0], cache_v[0], page_table, sb_bias[0])
    xs = _matmul(att_s, w_att_out, tm_s, residual=xs)

    xp = ffn(xp, 1)
    xs = ffn(xs, 1)

    y_prompt = _rmsnorm(xp, norm_final, _row_tile(xp.shape[0])).reshape(bp, tp, d)
    y_sample = _rmsnorm(xs, norm_final, _row_tile(xs.shape[0])).reshape(bs, ts, d)
    heads5 = lambda x, b, t: x.reshape(1, b, t, SB_HEADS, SB_HEAD_DIM)
    return (y_prompt, y_sample,
            shift_p[None], wkv_p[None],
            heads5(qkv_p[1], bp, tp), heads5(qkv_p[2], bp, tp),
            shift_s[None], wkv_s[None],
            heads5(qkv_s[1], bs, ts), heads5(qkv_s[2], bs, ts))
```

```python
import functools

import jax
import jax.numpy as jnp
from jax import lax
from jax.experimental import pallas as pl
from jax.experimental.pallas import tpu as pltpu

F32 = jnp.float32
BF16 = jnp.bfloat16

D_MODEL = 2048
RWKV_HEAD = 64
RWKV_HEADS = D_MODEL // RWKV_HEAD
HEAD_PAIRS = RWKV_HEADS // 2
LORA_PAD = 128
GN_EPS = 64e-5
RMS_EPS = 1e-6
SB_HEADS = 16
SB_HEAD_DIM = D_MODEL // SB_HEADS
Q_BLOCK = 128
PAGE_SIZE = 128
N_GROUPS = 4
EXPERTS_PER_GROUP = 8
N_EXPERTS = N_GROUPS * EXPERTS_PER_GROUP
D_EXPERT = 256
LANES = 128
SUBLANES = 8
VMEM_LIMIT_BYTES = 52 * 1024 * 1024


def _params(*semantics):
    return pltpu.CompilerParams(dimension_semantics=semantics, vmem_limit_bytes=VMEM_LIMIT_BYTES)


def _softplus(x):
    return jnp.maximum(x, 0.0) + jnp.log1p(jnp.exp(-jnp.abs(x)))


def _sigmoid(x):
    return 1.0 / (1.0 + jnp.exp(-x))


def _dot(a, b):
    return jnp.dot(a, b, preferred_element_type=F32)


def _dot_hilo(a, b_bf16):
    hi = a.astype(BF16)
    lo = (a - hi.astype(F32)).astype(BF16)
    return _dot(hi, b_bf16) + _dot(lo, b_bf16)


def _block_ones(n, seg):
    r = lax.broadcasted_iota(jnp.int32, (n, n), 0) // seg
    c = lax.broadcasted_iota(jnp.int32, (n, n), 1) // seg
    return jnp.where(r == c, 1.0, 0.0).astype(BF16)


def _rmsnorm_body(x_ref, g_ref, o_ref):
    x = x_ref[...]
    ms = jnp.mean(x * x, axis=-1, keepdims=True)
    o_ref[...] = x * lax.rsqrt(ms + RMS_EPS) * g_ref[...]


def _rmsnorm(x, g, tm):
    n, d = x.shape
    return pl.pallas_call(
        _rmsnorm_body,
        grid=(n // tm,),
        in_specs=[pl.BlockSpec((tm, d), lambda i: (i, 0)), pl.BlockSpec((1, d), lambda i: (0, 0))],
        out_specs=pl.BlockSpec((tm, d), lambda i: (i, 0)),
        out_shape=jax.ShapeDtypeStruct((n, d), F32),
        compiler_params=_params("parallel"),
        name="rmsnorm",
    )(x, g.reshape(1, d))


def _mm_body(x_ref, w_ref, o_ref):
    o_ref[...] = _dot(x_ref[...].astype(BF16), w_ref[...])


def _mm_res_body(x_ref, w_ref, r_ref, o_ref):
    o_ref[...] = r_ref[...] + _dot(x_ref[...].astype(BF16), w_ref[...])


def _matmul(x, w, tm, residual=None):
    n, k = x.shape
    m = w.shape[1]
    row = lambda i: (i, 0)
    in_specs = [pl.BlockSpec((tm, k), row), pl.BlockSpec((k, m), lambda i: (0, 0))]
    args = [x, w]
    body = _mm_body
    if residual is not None:
        in_specs.append(pl.BlockSpec((tm, m), row))
        args.append(residual)
        body = _mm_res_body
    return pl.pallas_call(
        body,
        grid=(n // tm,),
        in_specs=in_specs,
        out_specs=pl.BlockSpec((tm, m), row),
        out_shape=jax.ShapeDtypeStruct((n, m), F32),
        compiler_params=_params("parallel"),
        name="matmul",
    )(*args)


def _token_shift(h_ref, tail_ref, prev_ref, i, tm, seq_len):
    h = h_ref[...]
    if prev_ref is not None:
        return h, prev_ref[...]
    rolled = pltpu.roll(h, 1, 0)
    starts_sequence = (i * tm) % seq_len == 0
    first = jnp.where(starts_sequence, 0.0, tail_ref[SUBLANES - 1:SUBLANES, :])
    row = lax.broadcasted_iota(jnp.int32, h.shape, 0)
    return h, jnp.where(row == 0, first, rolled)


def _shift_specs(tm, d, has_prev, grid_rank):
    if grid_rank == 1:
        cur = lambda i: (i, 0)
        tail = lambda i: (jnp.maximum(i * (tm // SUBLANES) - 1, 0), 0)
    else:
        cur = lambda p, i: (i, 0)
        tail = lambda p, i: (jnp.maximum(i * (tm // SUBLANES) - 1, 0), 0)
    if has_prev:
        return [pl.BlockSpec((tm, d), cur), pl.BlockSpec((tm, d), cur)]
    return [pl.BlockSpec((tm, d), cur), pl.BlockSpec((SUBLANES, d), tail)]


def _rkv_body(h_ref, aux_ref, mu_ref, w_ref, o_ref, *, tm, seq_len, has_prev):
    i = pl.program_id(1)
    h, prev = _token_shift(h_ref, None if has_prev else aux_ref, aux_ref if has_prev else None, i, tm, seq_len)
    x = h + (prev - h) * mu_ref[0]
    o_ref[0] = _dot(x.astype(BF16), w_ref[0])


def _rwkv_rkv(h, prev, mu3, w3, tm, seq_len):
    n, d = h.shape
    has_prev = prev is not None
    return pl.pallas_call(
        functools.partial(_rkv_body, tm=tm, seq_len=seq_len, has_prev=has_prev),
        grid=(3, n // tm),
        in_specs=_shift_specs(tm, d, has_prev, 2) + [
            pl.BlockSpec((1, 1, d), lambda p, i: (p, 0, 0)),
            pl.BlockSpec((1, d, d), lambda p, i: (p, 0, 0)),
        ],
        out_specs=pl.BlockSpec((1, tm, d), lambda p, i: (p, i, 0)),
        out_shape=jax.ShapeDtypeStruct((3, n, d), F32),
        compiler_params=_params("arbitrary", "arbitrary"),
        name="rwkv_rkv",
    )(h, prev if has_prev else h, mu3, w3)


def _lora_body(h_ref, aux_ref, mu_ref, w0_ref, a0_ref, w1_ref, w2_ref, a1_ref, a2_ref, g1_ref, g2_ref,
               decay_ref, a_ref, g_ref, *, tm, seq_len, has_prev):
    i = pl.program_id(0)
    h, prev = _token_shift(h_ref, None if has_prev else aux_ref, aux_ref if has_prev else None, i, tm, seq_len)
    xx = prev - h
    xw = (h + xx * mu_ref[0]).astype(BF16)
    xa = (h + xx * mu_ref[1]).astype(BF16)
    xg = (h + xx * mu_ref[2]).astype(BF16)
    lw = _dot(jnp.tanh(_dot(xw, w1_ref[...])).astype(BF16), w2_ref[...])
    w = -_softplus(-(w0_ref[...] + lw)) - 0.5
    decay_ref[...] = -jnp.exp(w)
    la = _dot(_dot(xa, a1_ref[...]).astype(BF16), a2_ref[...])
    a_ref[...] = _sigmoid(a0_ref[...] + la)
    g_ref[...] = _dot(_sigmoid(_dot(xg, g1_ref[...])).astype(BF16), g2_ref[...])


def _rwkv_lora(h, prev, mu3, w0, a0, w1, w2, a1, a2, g1, g2, tm, seq_len):
    n, d = h.shape
    has_prev = prev is not None
    full = lambda arr: pl.BlockSpec(arr.shape, lambda i, nd=arr.ndim: (0,) * nd)
    row = pl.BlockSpec((tm, d), lambda i: (i, 0))
    out = jax.ShapeDtypeStruct((n, d), F32)
    return pl.pallas_call(
        functools.partial(_lora_body, tm=tm, seq_len=seq_len, has_prev=has_prev),
        grid=(n // tm,),
        in_specs=_shift_specs(tm, d, has_prev, 1) + [full(mu3), full(w0), full(a0), full(w1), full(w2),
                                                     full(a1), full(a2), full(g1), full(g2)],
        out_specs=[row, row, row],
        out_shape=[out, out, out],
        compiler_params=_params("parallel"),
        name="rwkv_lora",
    )(h, prev if has_prev else h, mu3, w0, a0, w1, w2, a1, a2, g1, g2)


STEP_SEQS = SUBLANES


def _segsum(x, ones_bd):
    return _dot_hilo(x, ones_bd)


def _wkv_step_body(r_ref, k_ref, v_ref, ld_ref, a_ref, g_ref, kkw_ref, kaw_ref, rkw_ref, lnw_ref, lnb_ref, s0_ref,
                   o_ref, st_ref):
    ones_bd = _block_ones(LANES, RWKV_HEAD)
    rows = lax.broadcasted_iota(jnp.int32, (RWKV_HEAD, LANES), 0)
    lanes = lax.broadcasted_iota(jnp.int32, (RWKV_HEAD, LANES), 1)
    diag = (lanes % RWKV_HEAD) == rows
    pairs = range(HEAD_PAIRS)
    lanes_of = [slice(p * LANES, (p + 1) * LANES) for p in pairs]
    r = [r_ref[:, sl] for sl in lanes_of]
    k = [k_ref[:, sl] for sl in lanes_of]
    v = [v_ref[:, sl] for sl in lanes_of]
    a = [a_ref[:, sl] for sl in lanes_of]
    d = [jnp.exp(ld_ref[:, sl]) for sl in lanes_of]
    kk_raw = [k[p] * kkw_ref[:, lanes_of[p]] for p in pairs]
    kk_norm = [jnp.sqrt(_segsum(x * x, ones_bd)) for x in kk_raw]
    kk = [kk_raw[p] / jnp.maximum(kk_norm[p], 1e-12) for p in pairs]
    kp = [k[p] * (1.0 + (a[p] - 1.0) * kaw_ref[:, lanes_of[p]]) for p in pairs]
    ka = [kk[p] * a[p] for p in pairs]
    dr = [d[p] * r[p] for p in pairs]
    c1 = [_segsum(ka[p] * r[p], ones_bd) for p in pairs]
    c2 = [_segsum(kp[p] * r[p], ones_bd) for p in pairs]
    y_rows = [[] for _ in pairs]
    for s in range(STEP_SEQS):
        row = lambda x: x[s:s + 1, :]
        state = [jnp.concatenate([s0_ref[s, 2 * p], s0_ref[s, 2 * p + 1]], axis=1) for p in pairs]
        s_kk = [_dot_hilo(state[p] * row(kk[p]), ones_bd) for p in pairs]
        y_part = [_dot_hilo(state[p] * row(dr[p]), ones_bd) for p in pairs]
        v_col = [_dot_hilo(jnp.where(diag, row(v[p]), 0.0), ones_bd) for p in pairs]
        for p in pairs:
            new = state[p] * row(d[p]) - s_kk[p] * row(ka[p]) + v_col[p] * row(kp[p])
            st_ref[s, 2 * p] = new[:, :RWKV_HEAD]
            st_ref[s, 2 * p + 1] = new[:, RWKV_HEAD:]
            y_col = y_part[p] - s_kk[p] * row(c1[p]) + v_col[p] * row(c2[p])
            y_rows[p].append(jnp.sum(jnp.where(diag, y_col, 0.0), axis=0, keepdims=True))
    y = [jnp.concatenate(y_rows[p], axis=0) for p in pairs]
    mean = [_segsum(x, ones_bd) * (1.0 / RWKV_HEAD) for x in y]
    yc = [y[p] - mean[p] for p in pairs]
    var = [_segsum(x * x, ones_bd) * (1.0 / RWKV_HEAD) for x in yc]
    bonus_dot = [_segsum(r[p] * kp[p] * rkw_ref[:, lanes_of[p]], ones_bd) for p in pairs]
    for p in pairs:
        sl = lanes_of[p]
        yn = yc[p] * lax.rsqrt(var[p] + GN_EPS) * lnw_ref[:, sl] + lnb_ref[:, sl]
        o_ref[:, sl] = (yn + bonus_dot[p] * v[p]) * g_ref[:, sl]


def _wkv_step(rkv, logd, a, g, kkw, kaw, rkw, lnw, lnb, s0):
    batch, d = logd.shape
    n = STEP_SEQS
    tok = lambda which: pl.BlockSpec((None, n, d), lambda i, which=which: (which, i, 0))
    tok2 = pl.BlockSpec((n, d), lambda i: (i, 0))
    vec = pl.BlockSpec((1, d), lambda i: (0, 0))
    state = pl.BlockSpec((n, RWKV_HEADS, RWKV_HEAD, RWKV_HEAD), lambda i: (i, 0, 0, 0))
    return pl.pallas_call(
        _wkv_step_body,
        grid=(batch // n,),
        in_specs=[tok(0), tok(1), tok(2), tok2, tok2, tok2, vec, vec, vec, vec, vec, state],
        out_specs=[tok2, state],
        out_shape=[jax.ShapeDtypeStruct((batch, d), F32),
                   jax.ShapeDtypeStruct((batch, RWKV_HEADS, RWKV_HEAD, RWKV_HEAD), F32)],
        compiler_params=_params("parallel"),
        name="wkv_step",
    )(rkv, rkv, rkv, logd, a, g, kkw, kaw, rkw, lnw, lnb, s0)


WKV_CHUNK = 64
WKV_STEP_PAIRS = 8
WKV_STEP_LANES = WKV_STEP_PAIRS * LANES


def _dot_nt(a, b):
    return lax.dot_general(a, b, (((1,), (1,)), ((), ())), preferred_element_type=F32)


def _split3(x):
    hi = x.astype(BF16)
    r1 = x - hi.astype(F32)
    mid = r1.astype(BF16)
    lo = (r1 - mid.astype(F32)).astype(BF16)
    return hi, mid, lo


def _wkv_chunk_body(r_ref, k_ref, v_ref, ld_ref, a_ref, g_ref, kkw_ref, kaw_ref, rkw_ref, lnw_ref, lnb_ref,
                    o_ref, st_ref, s_scr):
    t_chunk = pl.program_id(2)
    c = WKV_CHUNK
    ones_bd = _block_ones(LANES, RWKV_HEAD)
    row = lax.broadcasted_iota(jnp.int32, (LANES, LANES), 0)
    col = lax.broadcasted_iota(jnp.int32, (LANES, LANES), 1)
    strict = col < row
    incl = col <= row
    eye = jnp.where(row == col, 1.0, 0.0)
    ltri = jnp.where(incl, 1.0, 0.0).astype(BF16)
    head0 = lax.broadcasted_iota(jnp.int32, (c, LANES), 1) < RWKV_HEAD
    stack = lambda x: jnp.concatenate([jnp.where(head0, x, 0.0), jnp.where(head0, 0.0, x)], axis=0)

    @pl.when(t_chunk == 0)
    def _zero_state():
        s_scr[...] = jnp.zeros_like(s_scr)

    pairs = range(WKV_STEP_PAIRS)
    lanes_of = [slice(pp * LANES, (pp + 1) * LANES) for pp in pairs]
    each = lambda fn, *lists: [fn(*args) for args in zip(*lists)]
    bf = lambda x: x.astype(BF16)
    r = [r_ref[0, :, sl] for sl in lanes_of]
    k = [k_ref[0, :, sl] for sl in lanes_of]
    v = [v_ref[0, :, sl] for sl in lanes_of]
    ld = [ld_ref[0, :, sl] for sl in lanes_of]
    a = [a_ref[0, :, sl] for sl in lanes_of]
    kk_raw = each(lambda x, sl: x * kkw_ref[:, sl], k, lanes_of)
    kk_norm = each(lambda x: jnp.sqrt(_segsum(x * x, ones_bd)), kk_raw)
    kk = each(lambda x, n: x / jnp.maximum(n, 1e-12), kk_raw, kk_norm)
    kp = each(lambda x, y, sl: x * (1.0 + (y - 1.0) * kaw_ref[:, sl]), k, a, lanes_of)
    ka = each(lambda x, y: x * y, kk, a)
    ld_pieces = each(lambda x: _split3(jnp.concatenate([x, jnp.zeros_like(x)], axis=0)), ld)
    cum2 = [_dot(ltri, p3[0]) + _dot(ltri, p3[1]) + _dot(ltri, p3[2]) for p3 in ld_pieces]
    cum = [x[:c] for x in cum2]
    gam_end = [jnp.exp(x[c:]) for x in cum2]
    gam_inv = [jnp.exp(-x) for x in cum]
    at = each(lambda x, y: x * y, ka, gam_inv)
    kt = each(lambda x, y: x * y, kp, gam_inv)
    x_st = each(lambda kk_, r_, cum_, ld_: bf(jnp.concatenate(
        [stack(kk_ * jnp.exp(cum_ - ld_)), stack(r_ * jnp.exp(cum_))], axis=0)), kk, r, cum, ld)
    y_st = each(lambda x, y: bf(jnp.concatenate([stack(x), stack(y)], axis=0)), at, kt)
    gram = each(_dot_nt, x_st, y_st)
    m_a = [jnp.where(strict, x[:LANES, :LANES], 0.0) for x in gram]
    m_k = [bf(jnp.where(strict, x[:LANES, LANES:], 0.0)) for x in gram]
    n_a = [bf(jnp.where(incl, x[LANES:, :LANES], 0.0)) for x in gram]
    n_k = [bf(jnp.where(incl, x[LANES:, LANES:], 0.0)) for x in gram]
    power = [-x for x in m_a]
    inv = [eye + x for x in power]
    for _ in range(5):
        power = [_dot(bf(x), bf(x)) for x in power]
        inv = each(lambda t, p: t + _dot(bf(t), bf(p)), inv, power)
    inv = [bf(x) for x in inv]
    v_st = [stack(x) for x in v]
    v_stb = [bf(x) for x in v_st]
    v_stt = [bf(x.T) for x in v_st]
    at_end = each(lambda x, y: bf(stack(x * y)), at, gam_end)
    kt_end = each(lambda x, y: bf(stack(x * y)), kt, gam_end)
    s_from_v = each(_dot, v_stt, kt_end)
    m_kv = each(_dot, m_k, v_stb)
    n_kv = each(_dot, n_k, v_stb)
    s = [s_scr[pp] for pp in pairs]
    proj = each(lambda x, s_: _dot_nt(x, bf(s_)), x_st, s)
    u = each(lambda t, p, mv: -_dot(t, bf(p[:LANES] + mv)), inv, proj, m_kv)
    y2 = each(lambda p, n, u_, nv: p[LANES:] + _dot(n, bf(u_)) + nv, proj, n_a, u, n_kv)
    s_from_u = each(lambda u_, x: _dot(bf(u_.T), x), u, at_end)
    for pp in pairs:
        s_scr[pp] = s[pp] * jnp.concatenate([gam_end[pp], gam_end[pp]], axis=0) + s_from_u[pp] + s_from_v[pp]
    y = [x[:c] + x[c:] for x in y2]
    mean = [_segsum(x, ones_bd) * (1.0 / RWKV_HEAD) for x in y]
    yc = each(lambda x, m: x - m, y, mean)
    var = [_segsum(x * x, ones_bd) * (1.0 / RWKV_HEAD) for x in yc]
    bonus_dot = each(lambda r_, kp_, sl: _segsum(r_ * kp_ * rkw_ref[:, sl], ones_bd), r, kp, lanes_of)
    for pp in pairs:
        sl = lanes_of[pp]
        yn = yc[pp] * lax.rsqrt(var[pp] + GN_EPS) * lnw_ref[:, sl] + lnb_ref[:, sl]
        o_ref[0, :, sl] = (yn + bonus_dot[pp] * v[pp]) * g_ref[0, :, sl]

    @pl.when(t_chunk == pl.num_programs(2) - 1)
    def _store_state():
        for pp in range(WKV_STEP_PAIRS):
            s = s_scr[pp]
            st_ref[0, 2 * pp] = s[:RWKV_HEAD, :RWKV_HEAD]
            st_ref[0, 2 * pp + 1] = s[RWKV_HEAD:, RWKV_HEAD:]


def _wkv_chunked(rkv, logd, a, g, kkw, kaw, rkw, lnw, lnb, batch, seq_len):
    d = D_MODEL
    c = WKV_CHUNK
    w = WKV_STEP_LANES
    rkv4 = rkv.reshape(3, batch, seq_len, d)
    seq3 = lambda x: x.reshape(batch, seq_len, d)
    tok = lambda which: pl.BlockSpec((None, 1, c, w), lambda b, p, t, which=which: (which, b, t, p))
    tok3 = pl.BlockSpec((1, c, w), lambda b, p, t: (b, t, p))
    vec = pl.BlockSpec((1, w), lambda b, p, t: (0, p))
    state = pl.BlockSpec((1, 2 * WKV_STEP_PAIRS, RWKV_HEAD, RWKV_HEAD), lambda b, p, t: (b, p, 0, 0))
    o, st = pl.pallas_call(
        _wkv_chunk_body,
        grid=(batch, d // w, seq_len // c),
        in_specs=[tok(0), tok(1), tok(2), tok3, tok3, tok3, vec, vec, vec, vec, vec],
        out_specs=[tok3, state],
        out_shape=[jax.ShapeDtypeStruct((batch, seq_len, d), F32),
                   jax.ShapeDtypeStruct((batch, RWKV_HEADS, RWKV_HEAD, RWKV_HEAD), F32)],
        scratch_shapes=[pltpu.VMEM((WKV_STEP_PAIRS, LANES, LANES), F32)],
        compiler_params=_params("parallel", "parallel", "arbitrary"),
        name="wkv_chunked",
    )(rkv4, rkv4, rkv4, seq3(logd), seq3(a), seq3(g), kkw, kaw, rkw, lnw, lnb)
    return o.reshape(batch * seq_len, d), st


SB_STEP_HEADS = 8
SB_STEP_LANES = SB_STEP_HEADS * SB_HEAD_DIM


def _sb_prompt_body(q_ref, k_ref, v_ref, b_ref, o_ref, acc_scr, tail_scr):
    qi = pl.program_id(2)
    scale = SB_HEAD_DIM ** -0.5
    t_idx = lax.broadcasted_iota(jnp.int32, (Q_BLOCK, Q_BLOCK), 0)
    s_idx = lax.broadcasted_iota(jnp.int32, (Q_BLOCK, Q_BLOCK), 1)
    later = jnp.where(t_idx > s_idx, 1.0, 0.0).astype(BF16)
    later_and_ones = jnp.concatenate([later, jnp.ones((Q_BLOCK, Q_BLOCK), BF16)], axis=1)

    heads = range(SB_STEP_HEADS)
    lanes_of = [slice(hh * SB_HEAD_DIM, (hh + 1) * SB_HEAD_DIM) for hh in heads]

    def visit(kb, diagonal):
        start = pl.multiple_of(kb * Q_BLOCK, Q_BLOCK)
        keep = (lambda x: jnp.where(s_idx < t_idx, x, 0.0)) if diagonal else (lambda x: x)
        z = [_dot_nt(q_ref[0, :, sl].astype(BF16), k_ref[0, pl.ds(start, Q_BLOCK), sl].astype(BF16)) * scale
             + b_ref[:, sl] for sl in lanes_of]
        sp = [_softplus(x) for x in z]
        log_not = [keep(-x).astype(BF16) for x in sp]
        sums = [_dot(x, later_and_ones) for x in log_not]
        tail = [0.0 if diagonal else tail_scr[hh] for hh in heads]
        p = [keep(jnp.exp(z[hh] - sp[hh] + sums[hh][:, :Q_BLOCK] + tail[hh])).astype(BF16) for hh in heads]
        pv = [_dot(p[hh], v_ref[0, pl.ds(start, Q_BLOCK), lanes_of[hh]].astype(BF16)) for hh in heads]
        for hh in heads:
            acc_scr[hh] = pv[hh] if diagonal else acc_scr[hh] + pv[hh]
            tail_scr[hh] = tail[hh] + sums[hh][:, Q_BLOCK:]

    visit(qi, True)

    def body(j, carry):
        visit(qi - j, False)
        return carry

    lax.fori_loop(1, qi + 1, body, 0)
    for hh in heads:
        o_ref[0, :, hh * SB_HEAD_DIM:(hh + 1) * SB_HEAD_DIM] = acc_scr[hh]


def _sb_prompt(q, k, v, bias_cols, batch, seq_len):
    d = D_MODEL
    w = SB_STEP_LANES
    seq3 = lambda x: x.reshape(batch, seq_len, d)
    qspec = pl.BlockSpec((1, Q_BLOCK, w), lambda b, h, i: (b, i, h))
    kspec = pl.BlockSpec((1, seq_len, w), lambda b, h, i: (b, 0, h))
    vspec = kspec
    block_state = pltpu.VMEM((SB_STEP_HEADS, Q_BLOCK, SB_HEAD_DIM), F32)
    out = pl.pallas_call(
        _sb_prompt_body,
        grid=(batch, d // w, seq_len // Q_BLOCK),
        in_specs=[qspec, kspec, vspec, pl.BlockSpec((1, w), lambda b, h, i: (0, h))],
        out_specs=pl.BlockSpec((1, Q_BLOCK, w), lambda b, h, i: (b, i, h)),
        out_shape=jax.ShapeDtypeStruct((batch, seq_len, d), F32),
        scratch_shapes=[block_state, block_state],
        compiler_params=_params("parallel", "parallel", "arbitrary"),
        name="sb_prompt",
    )(seq3(q), seq3(k), seq3(v), bias_cols)
    return out.reshape(batch * seq_len, d)


PAGE_ROWS = PAGE_SIZE * SB_HEADS


SB_STEP_SEQS = 2


def _sb_sample_body(pt_ref, q_ref, kn_ref, vn_ref, *refs):
    n = SB_STEP_SEQS
    kc_refs, vc_refs = refs[:n], refs[n:2 * n]
    b_ref, col_ref, exp_ref, o_ref, acc_scr, tail_scr = refs[2 * n:]
    seqs = range(n)
    step = pl.program_id(1)
    n_steps = pl.num_programs(1)
    scale = SB_HEAD_DIM ** -0.5
    bias = b_ref[...]
    head = lax.broadcasted_iota(jnp.int32, (SB_HEADS, PAGE_ROWS), 0)
    own_head = lax.broadcasted_iota(jnp.int32, (SB_HEADS, PAGE_ROWS), 1) % SB_HEADS == head
    r_idx = lax.broadcasted_iota(jnp.int32, (PAGE_SIZE, PAGE_SIZE), 0)
    c_idx = lax.broadcasted_iota(jnp.int32, (PAGE_SIZE, PAGE_SIZE), 1)
    later = jnp.where(r_idx > c_idx, 1.0, 0.0).astype(BF16)
    ones = jnp.ones((PAGE_SIZE, PAGE_SIZE), BF16)

    @pl.when(step == 0)
    def _init():
        n_past = n_steps * PAGE_SIZE
        new_is_before = jnp.full((SB_HEADS, LANES), n_past, jnp.int32) < n_past
        for i in seqs:
            z_new = jnp.sum(q_ref[i] * kn_ref[i], axis=-1, keepdims=True) * scale + bias
            sp_new = _softplus(z_new)
            tail_scr[i] = jnp.where(new_is_before, -sp_new, 0.0)
            acc_scr[i] = jnp.where(new_is_before, jnp.exp(z_new - sp_new), 0.0) * vn_ref[i]

    z_all = [_dot_nt(q_ref[i].astype(BF16), kc_refs[i][0].astype(BF16)) for i in seqs]
    z_own = [_split3(jnp.where(own_head, x, 0.0)) for x in z_all]
    collapse = col_ref[...]
    z = [(_dot(p3[0], collapse) + _dot(p3[1], collapse) + _dot(p3[2], collapse)) * scale + bias for p3 in z_own]
    sp = [_softplus(x) for x in z]
    log_not = [(-x).astype(BF16) for x in sp]
    after = [_dot(x, later) for x in log_not]
    total = [_dot(x, ones) for x in log_not]
    tail = [tail_scr[i] for i in seqs]
    p = [jnp.exp(z[i] - sp[i] + after[i] + tail[i]).astype(BF16) for i in seqs]
    p_rows = [jnp.where(own_head, _dot(x, exp_ref[...]), 0.0).astype(BF16) for x in p]
    pv = [_dot(p_rows[i], vc_refs[i][0].astype(BF16)) for i in seqs]
    for i in seqs:
        acc_scr[i] += pv[i]
        tail_scr[i] = tail[i] + total[i]

    @pl.when(step == n_steps - 1)
    def _finish():
        o_ref[...] = acc_scr[...]


def _sb_sample(q, k_new, v_new, cache_k, cache_v, page_table, bias):
    batch, d = q.shape
    n_pages = page_table.shape[1]
    n_phys = cache_k.shape[0]
    pages = lambda c: c.reshape(n_phys, PAGE_ROWS, SB_HEAD_DIM)
    row_token = lax.broadcasted_iota(jnp.int32, (PAGE_ROWS, PAGE_SIZE), 0) // SB_HEADS
    collapse = (row_token == lax.broadcasted_iota(jnp.int32, (PAGE_ROWS, PAGE_SIZE), 1)).astype(BF16)
    bias_lanes = jnp.broadcast_to(bias[:, None], (SB_HEADS, LANES)).astype(F32)
    n = SB_STEP_SEQS
    tok = pl.BlockSpec((n, SB_HEADS, SB_HEAD_DIM), lambda b, s, pt: (b, 0, 0))
    page = lambda i: pl.BlockSpec((1, PAGE_ROWS, SB_HEAD_DIM),
                                  lambda b, s, pt, i=i: (pt[b * n + i, n_pages - 1 - s], 0, 0))
    const = lambda shape: pl.BlockSpec(shape, lambda b, s, pt: (0, 0))
    heads3 = lambda x: x.reshape(batch, SB_HEADS, SB_HEAD_DIM)
    per_seq = pltpu.VMEM((n, SB_HEADS, SB_HEAD_DIM), F32)
    out = pl.pallas_call(
        _sb_sample_body,
        grid_spec=pltpu.PrefetchScalarGridSpec(
            num_scalar_prefetch=1,
            grid=(batch // n, n_pages),
            in_specs=[tok, tok, tok] + [page(i) for i in range(n)] * 2 + [
                const((SB_HEADS, LANES)), const((PAGE_ROWS, PAGE_SIZE)), const((PAGE_SIZE, PAGE_ROWS))],
            out_specs=tok,
            scratch_shapes=[per_seq, per_seq],
        ),
        out_shape=jax.ShapeDtypeStruct((batch, SB_HEADS, SB_HEAD_DIM), F32),
        compiler_params=_params("parallel", "arbitrary"),
        name="sb_sample",
    )(page_table, heads3(q), heads3(k_new), heads3(v_new), *([pages(cache_k)] * n), *([pages(cache_v)] * n),
      bias_lanes, collapse, collapse.T)
    return out.reshape(batch, d)


ROUTER_OFFSET = N_GROUPS


def _router_body(x_ref, w_ref, b_ref, gate_ref, route_ref, count_ref, count_scr):
    logits = jnp.dot(x_ref[...], w_ref[...], preferred_element_type=F32,
                     precision=lax.Precision.HIGHEST) + b_ref[...]
    lane = lax.broadcasted_iota(jnp.int32, logits.shape, 1)
    neg = -jnp.inf
    first_lane_of = lambda hit: jnp.min(jnp.where(hit, lane, LANES), axis=-1, keepdims=True)
    is_group = lane < N_GROUPS
    glog = jnp.where(is_group, logits, neg)
    gmax = jnp.max(glog, axis=-1, keepdims=True)
    g_sel = first_lane_of(glog == gmax)
    p_group = 1.0 / jnp.sum(jnp.exp(glog - gmax), axis=-1, keepdims=True)
    in_group = ((lane - ROUTER_OFFSET) // EXPERTS_PER_GROUP == g_sel) & (lane >= ROUTER_OFFSET) \
        & (lane < ROUTER_OFFSET + N_EXPERTS)
    elog = jnp.where(in_group, logits, neg)
    top1 = jnp.max(elog, axis=-1, keepdims=True)
    lane1 = first_lane_of(elog == top1)
    rest = jnp.where(lane == lane1, neg, elog)
    top2 = jnp.max(rest, axis=-1, keepdims=True)
    lane2 = first_lane_of(rest == top2)
    e2 = jnp.exp(top2 - top1)
    w1 = p_group / (1.0 + e2)
    w2 = p_group * e2 / (1.0 + e2)
    gate_ref[...] = jnp.where(lane == lane1, w1, 0.0) + jnp.where(lane == lane2, w2, 0.0)
    @pl.when(pl.program_id(0) == 0)
    def _zero_counts():
        count_scr[...] = jnp.zeros_like(count_scr)

    tm = logits.shape[0]
    earlier = (lax.broadcasted_iota(jnp.int32, (tm, tm), 1)
               < lax.broadcasted_iota(jnp.int32, (tm, tm), 0)).astype(BF16)
    hot1 = jnp.where(lane == lane1, 1.0, 0.0)
    hot2 = jnp.where(lane == lane2, 1.0, 0.0)
    seen = count_scr[0:1, :]
    total1 = jnp.sum(hot1, axis=0, keepdims=True)
    rank1 = jnp.sum(hot1 * (seen + _dot(earlier, hot1.astype(BF16))), axis=-1, keepdims=True)
    rank2 = jnp.sum(hot2 * (seen + total1 + _dot(earlier, hot2.astype(BF16))), axis=-1, keepdims=True)
    seen = seen + total1 + jnp.sum(hot2, axis=0, keepdims=True)
    count_scr[...] = jnp.broadcast_to(seen, count_scr.shape)
    count_ref[...] = jnp.broadcast_to(seen, count_ref.shape)
    route_ref[...] = (jnp.where(lane == 0, (lane1 - ROUTER_OFFSET).astype(F32), 0.0)
                      + jnp.where(lane == 1, (lane2 - ROUTER_OFFSET).astype(F32), 0.0)
                      + jnp.where(lane == 2, w1, 0.0) + jnp.where(lane == 3, w2, 0.0)
                      + jnp.where(lane == 4, rank1, 0.0) + jnp.where(lane == 5, rank2, 0.0))


def _router(x, w_router, b_router, tm):
    n, d = x.shape
    out = jax.ShapeDtypeStruct((n, LANES), F32)
    row = pl.BlockSpec((tm, LANES), lambda i: (i, 0))
    counts = pl.BlockSpec((SUBLANES, LANES), lambda i: (0, 0))
    return pl.pallas_call(
        _router_body,
        grid=(n // tm,),
        in_specs=[pl.BlockSpec((tm, d), lambda i: (i, 0)),
                  pl.BlockSpec((d, LANES), lambda i: (0, 0)),
                  pl.BlockSpec((1, LANES), lambda i: (0, 0))],
        out_specs=[row, row, counts],
        out_shape=[out, out, jax.ShapeDtypeStruct((SUBLANES, LANES), F32)],
        scratch_shapes=[pltpu.VMEM((SUBLANES, LANES), F32)],
        compiler_params=_params("arbitrary"),
        name="moe_router",
    )(x, w_router, b_router)


def _moe_body(x_ref, gate_ref, res_ref, wg_ref, wu_ref, wd_ref, o_ref, xb_scr):
    e = pl.program_id(1)

    @pl.when(e == 0)
    def _init():
        xb_scr[...] = x_ref[...].astype(BF16)
        o_ref[...] = res_ref[...]

    xb = xb_scr[...]
    hg = _dot(xb, wg_ref[0].astype(BF16))
    hu = _dot(xb, wu_ref[0].astype(BF16))
    gate = gate_ref[...]
    lane = lax.broadcasted_iota(jnp.int32, gate.shape, 1)
    ge = jnp.sum(jnp.where(lane == e + ROUTER_OFFSET, gate, 0.0), axis=-1, keepdims=True)
    act = (hg * _sigmoid(hg)) * hu * ge
    o_ref[...] += _dot(act.astype(BF16), wd_ref[0].astype(BF16))


def _moe_dense(x, gate, residual, wg, wu, wd, layer, tm):
    n, d = x.shape
    row = lambda i, e: (i, 0)
    return pl.pallas_call(
        _moe_body,
        grid=(n // tm, N_EXPERTS),
        in_specs=[pl.BlockSpec((tm, d), row), pl.BlockSpec((tm, LANES), row), pl.BlockSpec((tm, d), row),
                  pl.BlockSpec((None, 1, d, D_EXPERT), lambda i, e: (layer, e, 0, 0)),
                  pl.BlockSpec((None, 1, d, D_EXPERT), lambda i, e: (layer, e, 0, 0)),
                  pl.BlockSpec((None, 1, D_EXPERT, d), lambda i, e: (layer, e, 0, 0))],
        out_specs=pl.BlockSpec((tm, d), row),
        out_shape=jax.ShapeDtypeStruct((n, d), F32),
        scratch_shapes=[pltpu.VMEM((tm, d), BF16)],
        compiler_params=_params("parallel", "arbitrary"),
        name="moe_experts",
    )(x, gate, residual, wg, wu, wd)


MOE_TILE = 256
COMBINE_TILE = 256


def _row_copy(src_hbm, index_ref, base, buf, sem):
    def copy(r):
        return pltpu.make_async_copy(src_hbm.at[pl.ds(index_ref[base + r], 1)], buf.at[pl.ds(r, 1)], sem)
    return copy


def _for_rows(n_rows, fn):
    def body(r, carry):
        fn(r)
        return carry
    lax.fori_loop(0, n_rows, body, 0, unroll=8)


def _gather_pipeline(step, n_steps, copies_of, n_rows):
    slot = step % 2

    @pl.when(step == 0)
    def _prime():
        _for_rows(n_rows, lambda r: copies_of(0, 0)(r).start())

    @pl.when(step + 1 < n_steps)
    def _prefetch():
        _for_rows(n_rows, lambda r: copies_of(step + 1, 1 - slot)(r).start())

    _for_rows(n_rows, lambda r: copies_of(step, slot)(r).wait())
    return slot


def _moe_dispatch_body(pos_ref, h_ref, zeros_hbm, xs_hbm, sem):
    del zeros_hbm
    i = pl.program_id(0)
    tm = h_ref.shape[0]

    def copy(j):
        r = j % tm
        return pltpu.make_async_copy(h_ref.at[pl.ds(r, 1)], xs_hbm.at[pl.ds(pos_ref[i * 2 * tm + j], 1)], sem)

    _for_rows(2 * tm, lambda j: copy(j).start())
    _for_rows(2 * tm, lambda j: copy(j).wait())


def _moe_sparse_body(tile_expert_ref, n_used_ref, x_ref, wg_ref, wu_ref, wd_ref, y_ref, wg_bf, wu_bf, wd_bf):
    t = pl.program_id(0)

    @pl.when(jnp.logical_or(t == 0, tile_expert_ref[t] != tile_expert_ref[jnp.maximum(t - 1, 0)]))
    def _new_expert():
        wg_bf[...] = wg_ref[0].astype(BF16)
        wu_bf[...] = wu_ref[0].astype(BF16)
        wd_bf[...] = wd_ref[0].astype(BF16)

    @pl.when(t < n_used_ref[0])
    def _compute():
        x = x_ref[...].astype(BF16)
        hg = _dot(x, wg_bf[...])
        hu = _dot(x, wu_bf[...])
        y_ref[...] = _dot(((hg * _sigmoid(hg)) * hu).astype(BF16), wd_bf[...])

    @pl.when(t >= n_used_ref[0])
    def _unused_tile():
        y_ref[...] = jnp.zeros_like(y_ref)


def _moe_combine_body(pos_ref, x_ref, route_ref, y_hbm, o_ref, ybuf, sem):
    i = pl.program_id(0)
    tm = x_ref.shape[0]
    copies_of = lambda tile, s: _row_copy(y_hbm, pos_ref, tile * 2 * tm, ybuf.at[s], sem.at[s])
    slot = _gather_pipeline(i, pl.num_programs(0), copies_of, 2 * tm)
    rows = ybuf[slot]
    route = route_ref[...]
    o_ref[...] = x_ref[...] + route[:, 2:3] * rows[:tm] + route[:, 3:4] * rows[tm:]


def _route_plan(route, counts, tile, tm):
    n = route.shape[0]
    counts = counts[0, ROUTER_OFFSET:ROUTER_OFFSET + N_EXPERTS].astype(jnp.int32)
    padded = (counts + tile - 1) // tile * tile
    ends = jnp.cumsum(padded)
    offs = (ends - padded).astype(F32)
    expert = route[:, 0:2]
    hot = expert[:, :, None] == jnp.arange(N_EXPERTS, dtype=F32)[None, None, :]
    pos = (jnp.sum(jnp.where(hot, offs[None, None, :], 0.0), axis=-1) + route[:, 4:6]).astype(jnp.int32)
    n_tiles = (2 * n + N_EXPERTS * tile) // tile
    tile_start = jnp.arange(n_tiles, dtype=jnp.int32) * tile
    tile_expert = jnp.minimum(jnp.sum((ends[None, :] <= tile_start[:, None]).astype(jnp.int32), axis=1),
                              N_EXPERTS - 1)
    n_used = (ends[-1] // tile).astype(jnp.int32).reshape(1)
    pos_tiled = pos.reshape(n // tm, tm, 2).transpose(0, 2, 1).reshape(2 * n)
    return pos_tiled, tile_expert, n_used


def _moe_sparse(h, route, counts, residual, wg, wu, wd, layer):
    n, d = h.shape
    tile, tm = MOE_TILE, COMBINE_TILE
    pos_tiled, tile_expert, n_used = _route_plan(route, counts, tile, tm)
    p_rows = 2 * n + N_EXPERTS * tile
    token_tile = pl.BlockSpec((tm, d), lambda i, pos: (i, 0))
    xs = pl.pallas_call(
        _moe_dispatch_body,
        grid_spec=pltpu.PrefetchScalarGridSpec(
            num_scalar_prefetch=1,
            grid=(n // tm,),
            in_specs=[token_tile, pl.BlockSpec(memory_space=pl.ANY)],
            out_specs=pl.BlockSpec(memory_space=pl.ANY),
            scratch_shapes=[pltpu.SemaphoreType.DMA(())],
        ),
        out_shape=jax.ShapeDtypeStruct((p_rows, d), F32),
        input_output_aliases={2: 0},
        compiler_params=_params("arbitrary"),
        name="moe_dispatch",
    )(pos_tiled, h, jnp.zeros((p_rows, d), F32))
    y = pl.pallas_call(
        _moe_sparse_body,
        grid_spec=pltpu.PrefetchScalarGridSpec(
            num_scalar_prefetch=2,
            grid=(p_rows // tile,),
            in_specs=[pl.BlockSpec((tile, d), lambda t, te, nu: (t, 0)),
                      pl.BlockSpec((None, 1, d, D_EXPERT), lambda t, te, nu: (layer, te[t], 0, 0)),
                      pl.BlockSpec((None, 1, d, D_EXPERT), lambda t, te, nu: (layer, te[t], 0, 0)),
                      pl.BlockSpec((None, 1, D_EXPERT, d), lambda t, te, nu: (layer, te[t], 0, 0))],
            out_specs=pl.BlockSpec((tile, d), lambda t, te, nu: (t, 0)),
            scratch_shapes=[pltpu.VMEM((d, D_EXPERT), BF16), pltpu.VMEM((d, D_EXPERT), BF16),
                            pltpu.VMEM((D_EXPERT, d), BF16)],
        ),
        out_shape=jax.ShapeDtypeStruct((p_rows, d), F32),
        compiler_params=_params("arbitrary"),
        name="moe_sparse_experts",
    )(tile_expert, n_used, xs, wg, wu, wd)
    return pl.pallas_call(
        _moe_combine_body,
        grid_spec=pltpu.PrefetchScalarGridSpec(
            num_scalar_prefetch=1,
            grid=(n // tm,),
            in_specs=[token_tile, pl.BlockSpec((tm, LANES), lambda i, pos: (i, 0)),
                      pl.BlockSpec(memory_space=pl.ANY)],
            out_specs=token_tile,
            scratch_shapes=[pltpu.VMEM((2, 2 * tm, d), F32), pltpu.SemaphoreType.DMA((2,))],
        ),
        out_shape=jax.ShapeDtypeStruct((n, d), F32),
        compiler_params=_params("arbitrary"),
        name="moe_combine",
    )(pos_tiled, residual, route, y)


def _row_tile(n):
    return 256 if n % 256 == 0 else LANES


def _ffn(x, norm_g, w_router, b_router, wg, wu, wd, layer):
    n = x.shape[0]
    tm = _row_tile(n)
    h = _rmsnorm(x, norm_g, tm)
    gate, route, counts = _router(h, w_router, b_router, tm)
    if n % COMBINE_TILE == 0 and n >= N_EXPERTS * MOE_TILE:
        return _moe_sparse(h, route, counts, x, wg, wu, wd, layer)
    return _moe_dense(h, gate, x, wg, wu, wd, layer, LANES)


def kernel(x_prompt, x_sample, state_tshift, state_wkv, cache_k, cache_v, page_table, norm_mix, norm_ffn, norm_final, rwkv_mu, rwkv_w0, rwkv_w1, rwkv_w2, rwkv_a0, rwkv_a1, rwkv_a2, rwkv_g1, rwkv_g2, rwkv_kk, rwkv_ka, rwkv_rk, rwkv_wr, rwkv_wk, rwkv_wv, rwkv_wo, rwkv_lnw, rwkv_lnb, sb_wqkv, sb_wo, sb_bias, moe_wgroup, moe_bgroup, moe_wexpert, moe_bexpert, moe_wgate, moe_wup, moe_wdown):
    d = D_MODEL
    bp, tp, _ = x_prompt.shape
    bs, ts, _ = x_sample.shape
    xp = x_prompt.reshape(bp * tp, d)
    xs = x_sample.reshape(bs * ts, d)
    row1 = lambda v: v.reshape(1, d)
    pad_cols = lambda w: jnp.pad(w, ((0, 0), (0, LORA_PAD - w.shape[1]))).astype(BF16)
    pad_rows = lambda w: jnp.pad(w, ((0, LORA_PAD - w.shape[0]), (0, 0))).astype(BF16)

    mu = rwkv_mu[0]
    mu_rkv = jnp.stack([mu[0], mu[2], mu[3]]).reshape(3, 1, d)
    mu_lora = jnp.stack([mu[1], mu[4], mu[5]]).reshape(3, 1, d)
    w_rkv = jnp.stack([rwkv_wr[0], rwkv_wk[0], rwkv_wv[0]]).astype(BF16)
    lora = (row1(rwkv_w0[0]), row1(rwkv_a0[0]), pad_cols(rwkv_w1[0]), pad_rows(rwkv_w2[0]),
            pad_cols(rwkv_a1[0]), pad_rows(rwkv_a2[0]), rwkv_g1[0].astype(BF16), rwkv_g2[0].astype(BF16))
    head_vecs = (row1(rwkv_kk[0]), row1(rwkv_ka[0]), row1(rwkv_rk[0]), row1(rwkv_lnw[0]), row1(rwkv_lnb[0]))
    wo = rwkv_wo[0].astype(BF16)

    def rwkv_layer(x, prev, s0, batch, seq_len):
        tm = _row_tile(x.shape[0])
        h = _rmsnorm(x, norm_mix[0], tm)
        rkv = _rwkv_rkv(h, prev, mu_rkv, w_rkv, tm, seq_len)
        logd, a, g = _rwkv_lora(h, prev, mu_lora, *lora, tm, seq_len)
        if s0 is None:
            o, s_final = _wkv_chunked(rkv, logd, a, g, *head_vecs, batch, seq_len)
        else:
            assert seq_len == 1, "a given initial state is only supported for single-token sequences"
            o, s_final = _wkv_step(rkv, logd, a, g, *head_vecs, s0)
        return _matmul(o, wo, tm, residual=x), h, s_final

    xp, hp, wkv_p = rwkv_layer(xp, None, None, bp, tp)
    xs, hs, wkv_s = rwkv_layer(xs, state_tshift[0], state_wkv[0], bs, ts)
    shift_p = hp.reshape(bp, tp, d)[:, -1]
    shift_s = hs.reshape(bs, ts, d)[:, -1]

    def ffn(x, i):
        w_router = jnp.pad(jnp.concatenate([moe_wgroup[i], moe_wexpert[i]], axis=1),
                           ((0, 0), (0, LANES - N_GROUPS - N_EXPERTS)))
        b_router = jnp.pad(jnp.concatenate([moe_bgroup[i], moe_bexpert[i]]),
                           (0, LANES - N_GROUPS - N_EXPERTS)).reshape(1, LANES)
        return _ffn(x, norm_ffn[i], w_router, b_router, moe_wgate, moe_wup, moe_wdown, i)

    xp = ffn(xp, 0)
    xs = ffn(xs, 0)

    w_qkv = jnp.stack(jnp.split(sb_wqkv[0], 3, axis=-1)).astype(BF16)
    w_att_out = sb_wo[0].astype(BF16)
    bias_cols = jnp.repeat(sb_bias[0].astype(F32), SB_HEAD_DIM).reshape(1, d)

    def qkv_proj(x):
        tm = _row_tile(x.shape[0])
        h = _rmsnorm(x, norm_mix[1], tm)
        return [_matmul(h, w_qkv[j], tm) for j in range(3)], tm

    qkv_p, tm_p = qkv_proj(xp)
    att_p = _sb_prompt(*qkv_p, bias_cols, bp, tp)
    xp = _matmul(att_p, w_att_out, tm_p, residual=xp)

    qkv_s, tm_s = qkv_proj(xs)
    att_s = _sb_sample(qkv_s[0], qkv_s[1], qkv_s[2], cache_k[0], cache_v[0], page_table, sb_bias[0])
    xs = _matmul(att_s, w_att_out, tm_s, residual=xs)

    xp = ffn(xp, 1)
    xs = ffn(xs, 1)

    y_prompt = _rmsnorm(xp, norm_final, _row_tile(xp.shape[0])).reshape(bp, tp, d)
    y_sample = _rmsnorm(xs, norm_final, _row_tile(xs.shape[0])).reshape(bs, ts, d)
    heads5 = lambda x, b, t: x.reshape(1, b, t, SB_HEADS, SB_HEAD_DIM)
    return (y_prompt, y_sample,
            shift_p[None], wkv_p[None],
            heads5(qkv_p[1], bp, tp), heads5(qkv_p[2], bp, tp),
            shift_s[None], wkv_s[None],
            heads5(qkv_s[1], bs, ts), heads5(qkv_s[2], bs, ts))
```

```python
import functools

import jax
import jax.numpy as jnp
from jax import lax
from jax.experimental import pallas as pl
from jax.experimental.pallas import tpu as pltpu

F32 = jnp.float32
BF16 = jnp.bfloat16

D_MODEL = 2048
RWKV_HEAD = 64
RWKV_HEADS = D_MODEL // RWKV_HEAD
HEAD_PAIRS = RWKV_HEADS // 2
LORA_PAD = 128
GN_EPS = 64e-5
RMS_EPS = 1e-6
SB_HEADS = 16
SB_HEAD_DIM = D_MODEL // SB_HEADS
Q_BLOCK = 128
PAGE_SIZE = 128
N_GROUPS = 4
EXPERTS_PER_GROUP = 8
N_EXPERTS = N_GROUPS * EXPERTS_PER_GROUP
D_EXPERT = 256
LANES = 128
SUBLANES = 8
VMEM_LIMIT_BYTES = 52 * 1024 * 1024


def _params(*semantics):
    return pltpu.CompilerParams(dimension_semantics=semantics, vmem_limit_bytes=VMEM_LIMIT_BYTES)


def _softplus(x):
    return jnp.maximum(x, 0.0) + jnp.log(1.0 + jnp.exp(-jnp.abs(x)))


def _sigmoid(x):
    return 1.0 / (1.0 + jnp.exp(-x))


def _dot(a, b):
    return jnp.dot(a, b, preferred_element_type=F32)


def _dot_hilo(a, b_bf16):
    hi = a.astype(BF16)
    lo = (a - hi.astype(F32)).astype(BF16)
    return _dot(hi, b_bf16) + _dot(lo, b_bf16)


def _block_ones(n, seg):
    r = lax.broadcasted_iota(jnp.int32, (n, n), 0) // seg
    c = lax.broadcasted_iota(jnp.int32, (n, n), 1) // seg
    return jnp.where(r == c, 1.0, 0.0).astype(BF16)


def _rmsnorm_body(x_ref, g_ref, o_ref):
    x = x_ref[...]
    ms = jnp.mean(x * x, axis=-1, keepdims=True)
    o_ref[...] = x * lax.rsqrt(ms + RMS_EPS) * g_ref[...]


def _rmsnorm(x, g, tm):
    n, d = x.shape
    return pl.pallas_call(
        _rmsnorm_body,
        grid=(n // tm,),
        in_specs=[pl.BlockSpec((tm, d), lambda i: (i, 0)), pl.BlockSpec((1, d), lambda i: (0, 0))],
        out_specs=pl.BlockSpec((tm, d), lambda i: (i, 0)),
        out_shape=jax.ShapeDtypeStruct((n, d), F32),
        compiler_params=_params("parallel"),
        name="rmsnorm",
    )(x, g.reshape(1, d))


def _mm_body(x_ref, w_ref, o_ref):
    o_ref[...] = _dot(x_ref[...].astype(BF16), w_ref[...])


def _mm_res_body(x_ref, w_ref, r_ref, o_ref):
    o_ref[...] = r_ref[...] + _dot(x_ref[...].astype(BF16), w_ref[...])


def _matmul(x, w, tm, residual=None):
    n, k = x.shape
    m = w.shape[1]
    row = lambda i: (i, 0)
    in_specs = [pl.BlockSpec((tm, k), row), pl.BlockSpec((k, m), lambda i: (0, 0))]
    args = [x, w]
    body = _mm_body
    if residual is not None:
        in_specs.append(pl.BlockSpec((tm, m), row))
        args.append(residual)
        body = _mm_res_body
    return pl.pallas_call(
        body,
        grid=(n // tm,),
        in_specs=in_specs,
        out_specs=pl.BlockSpec((tm, m), row),
        out_shape=jax.ShapeDtypeStruct((n, m), F32),
        compiler_params=_params("parallel"),
        name="matmul",
    )(*args)


def _token_shift(h_ref, tail_ref, prev_ref, i, tm, seq_len):
    h = h_ref[...]
    if prev_ref is not None:
        return h, prev_ref[...]
    rolled = pltpu.roll(h, 1, 0)
    starts_sequence = (i * tm) % seq_len == 0
    first = jnp.where(starts_sequence, 0.0, tail_ref[SUBLANES - 1:SUBLANES, :])
    row = lax.broadcasted_iota(jnp.int32, h.shape, 0)
    return h, jnp.where(row == 0, first, rolled)


def _shift_specs(tm, d, has_prev, grid_rank):
    if grid_rank == 1:
        cur = lambda i: (i, 0)
        tail = lambda i: (jnp.maximum(i * (tm // SUBLANES) - 1, 0), 0)
    else:
        cur = lambda p, i: (i, 0)
        tail = lambda p, i: (jnp.maximum(i * (tm // SUBLANES) - 1, 0), 0)
    if has_prev:
        return [pl.BlockSpec((tm, d), cur), pl.BlockSpec((tm, d), cur)]
    return [pl.BlockSpec((tm, d), cur), pl.BlockSpec((SUBLANES, d), tail)]


def _rkv_body(h_ref, aux_ref, mu_ref, w_ref, o_ref, *, tm, seq_len, has_prev):
    i = pl.program_id(1)
    h, prev = _token_shift(h_ref, None if has_prev else aux_ref, aux_ref if has_prev else None, i, tm, seq_len)
    x = h + (prev - h) * mu_ref[0]
    o_ref[0] = _dot(x.astype(BF16), w_ref[0])


def _rwkv_rkv(h, prev, mu3, w3, tm, seq_len):
    n, d = h.shape
    has_prev = prev is not None
    return pl.pallas_call(
        functools.partial(_rkv_body, tm=tm, seq_len=seq_len, has_prev=has_prev),
        grid=(3, n // tm),
        in_specs=_shift_specs(tm, d, has_prev, 2) + [
            pl.BlockSpec((1, 1, d), lambda p, i: (p, 0, 0)),
            pl.BlockSpec((1, d, d), lambda p, i: (p, 0, 0)),
        ],
        out_specs=pl.BlockSpec((1, tm, d), lambda p, i: (p, i, 0)),
        out_shape=jax.ShapeDtypeStruct((3, n, d), F32),
        compiler_params=_params("arbitrary", "arbitrary"),
        name="rwkv_rkv",
    )(h, prev if has_prev else h, mu3, w3)


def _lora_body(h_ref, aux_ref, mu_ref, w0_ref, a0_ref, w1_ref, w2_ref, a1_ref, a2_ref, g1_ref, g2_ref,
               decay_ref, a_ref, g_ref, *, tm, seq_len, has_prev):
    i = pl.program_id(0)
    h, prev = _token_shift(h_ref, None if has_prev else aux_ref, aux_ref if has_prev else None, i, tm, seq_len)
    xx = prev - h
    xw = (h + xx * mu_ref[0]).astype(BF16)
    xa = (h + xx * mu_ref[1]).astype(BF16)
    xg = (h + xx * mu_ref[2]).astype(BF16)
    lw = _dot(jnp.tanh(_dot(xw, w1_ref[...])).astype(BF16), w2_ref[...])
    w = -_softplus(-(w0_ref[...] + lw)) - 0.5
    decay_ref[...] = -jnp.exp(w)
    la = _dot(_dot(xa, a1_ref[...]).astype(BF16), a2_ref[...])
    a_ref[...] = _sigmoid(a0_ref[...] + la)
    g_ref[...] = _dot(_sigmoid(_dot(xg, g1_ref[...])).astype(BF16), g2_ref[...])


def _rwkv_lora(h, prev, mu3, w0, a0, w1, w2, a1, a2, g1, g2, tm, seq_len):
    n, d = h.shape
    has_prev = prev is not None
    full = lambda arr: pl.BlockSpec(arr.shape, lambda i, nd=arr.ndim: (0,) * nd)
    row = pl.BlockSpec((tm, d), lambda i: (i, 0))
    out = jax.ShapeDtypeStruct((n, d), F32)
    return pl.pallas_call(
        functools.partial(_lora_body, tm=tm, seq_len=seq_len, has_prev=has_prev),
        grid=(n // tm,),
        in_specs=_shift_specs(tm, d, has_prev, 1) + [full(mu3), full(w0), full(a0), full(w1), full(w2),
                                                     full(a1), full(a2), full(g1), full(g2)],
        out_specs=[row, row, row],
        out_shape=[out, out, out],
        compiler_params=_params("parallel"),
        name="rwkv_lora",
    )(h, prev if has_prev else h, mu3, w0, a0, w1, w2, a1, a2, g1, g2)


STEP_SEQS = SUBLANES


def _segsum(x, ones_bd):
    return _dot_hilo(x, ones_bd)


def _wkv_step_body(r_ref, k_ref, v_ref, ld_ref, a_ref, g_ref, kkw_ref, kaw_ref, rkw_ref, lnw_ref, lnb_ref, s0_ref,
                   o_ref, st_ref):
    ones_bd = _block_ones(LANES, RWKV_HEAD)
    rows = lax.broadcasted_iota(jnp.int32, (RWKV_HEAD, LANES), 0)
    lanes = lax.broadcasted_iota(jnp.int32, (RWKV_HEAD, LANES), 1)
    diag = (lanes % RWKV_HEAD) == rows
    pairs = range(HEAD_PAIRS)
    lanes_of = [slice(p * LANES, (p + 1) * LANES) for p in pairs]
    r = [r_ref[:, sl] for sl in lanes_of]
    k = [k_ref[:, sl] for sl in lanes_of]
    v = [v_ref[:, sl] for sl in lanes_of]
    a = [a_ref[:, sl] for sl in lanes_of]
    d = [jnp.exp(ld_ref[:, sl]) for sl in lanes_of]
    kk_raw = [k[p] * kkw_ref[:, lanes_of[p]] for p in pairs]
    kk_norm = [jnp.sqrt(_segsum(x * x, ones_bd)) for x in kk_raw]
    kk = [kk_raw[p] / jnp.maximum(kk_norm[p], 1e-12) for p in pairs]
    kp = [k[p] * (1.0 + (a[p] - 1.0) * kaw_ref[:, lanes_of[p]]) for p in pairs]
    ka = [kk[p] * a[p] for p in pairs]
    dr = [d[p] * r[p] for p in pairs]
    c1 = [_segsum(ka[p] * r[p], ones_bd) for p in pairs]
    c2 = [_segsum(kp[p] * r[p], ones_bd) for p in pairs]
    y_rows = [[] for _ in pairs]
    for s in range(STEP_SEQS):
        row = lambda x: x[s:s + 1, :]
        state = [jnp.concatenate([s0_ref[s, 2 * p], s0_ref[s, 2 * p + 1]], axis=1) for p in pairs]
        s_kk = [_dot_hilo(state[p] * row(kk[p]), ones_bd) for p in pairs]
        y_part = [_dot_hilo(state[p] * row(dr[p]), ones_bd) for p in pairs]
        v_col = [_dot_hilo(jnp.where(diag, row(v[p]), 0.0), ones_bd) for p in pairs]
        for p in pairs:
            new = state[p] * row(d[p]) - s_kk[p] * row(ka[p]) + v_col[p] * row(kp[p])
            st_ref[s, 2 * p] = new[:, :RWKV_HEAD]
            st_ref[s, 2 * p + 1] = new[:, RWKV_HEAD:]
            y_col = y_part[p] - s_kk[p] * row(c1[p]) + v_col[p] * row(c2[p])
            y_rows[p].append(jnp.sum(jnp.where(diag, y_col, 0.0), axis=0, keepdims=True))
    y = [jnp.concatenate(y_rows[p], axis=0) for p in pairs]
    mean = [_segsum(x, ones_bd) * (1.0 / RWKV_HEAD) for x in y]
    yc = [y[p] - mean[p] for p in pairs]
    var = [_segsum(x * x, ones_bd) * (1.0 / RWKV_HEAD) for x in yc]
    bonus_dot = [_segsum(r[p] * kp[p] * rkw_ref[:, lanes_of[p]], ones_bd) for p in pairs]
    for p in pairs:
        sl = lanes_of[p]
        yn = yc[p] * lax.rsqrt(var[p] + GN_EPS) * lnw_ref[:, sl] + lnb_ref[:, sl]
        o_ref[:, sl] = (yn + bonus_dot[p] * v[p]) * g_ref[:, sl]


def _wkv_step(rkv, logd, a, g, kkw, kaw, rkw, lnw, lnb, s0):
    batch, d = logd.shape
    n = STEP_SEQS
    tok = lambda which: pl.BlockSpec((None, n, d), lambda i, which=which: (which, i, 0))
    tok2 = pl.BlockSpec((n, d), lambda i: (i, 0))
    vec = pl.BlockSpec((1, d), lambda i: (0, 0))
    state = pl.BlockSpec((n, RWKV_HEADS, RWKV_HEAD, RWKV_HEAD), lambda i: (i, 0, 0, 0))
    return pl.pallas_call(
        _wkv_step_body,
        grid=(batch // n,),
        in_specs=[tok(0), tok(1), tok(2), tok2, tok2, tok2, vec, vec, vec, vec, vec, state],
        out_specs=[tok2, state],
        out_shape=[jax.ShapeDtypeStruct((batch, d), F32),
                   jax.ShapeDtypeStruct((batch, RWKV_HEADS, RWKV_HEAD, RWKV_HEAD), F32)],
        compiler_params=_params("parallel"),
        name="wkv_step",
    )(rkv, rkv, rkv, logd, a, g, kkw, kaw, rkw, lnw, lnb, s0)


WKV_CHUNK = 64
WKV_STEP_PAIRS = 8
WKV_STEP_LANES = WKV_STEP_PAIRS * LANES


def _dot_nt(a, b):
    return lax.dot_general(a, b, (((1,), (1,)), ((), ())), preferred_element_type=F32)


def _split3(x):
    hi = x.astype(BF16)
    r1 = x - hi.astype(F32)
    mid = r1.astype(BF16)
    lo = (r1 - mid.astype(F32)).astype(BF16)
    return hi, mid, lo


def _wkv_chunk_body(r_ref, k_ref, v_ref, ld_ref, a_ref, g_ref, kkw_ref, kaw_ref, rkw_ref, lnw_ref, lnb_ref,
                    o_ref, st_ref, s_scr):
    t_chunk = pl.program_id(2)
    c = WKV_CHUNK
    ones_bd = _block_ones(LANES, RWKV_HEAD)
    row = lax.broadcasted_iota(jnp.int32, (LANES, LANES), 0)
    col = lax.broadcasted_iota(jnp.int32, (LANES, LANES), 1)
    strict = col < row
    incl = col <= row
    eye = jnp.where(row == col, 1.0, 0.0)
    ltri = jnp.where(incl, 1.0, 0.0).astype(BF16)
    head0 = lax.broadcasted_iota(jnp.int32, (c, LANES), 1) < RWKV_HEAD
    stack = lambda x: jnp.concatenate([jnp.where(head0, x, 0.0), jnp.where(head0, 0.0, x)], axis=0)

    @pl.when(t_chunk == 0)
    def _zero_state():
        s_scr[...] = jnp.zeros_like(s_scr)

    pairs = range(WKV_STEP_PAIRS)
    lanes_of = [slice(pp * LANES, (pp + 1) * LANES) for pp in pairs]
    each = lambda fn, *lists: [fn(*args) for args in zip(*lists)]
    bf = lambda x: x.astype(BF16)
    r = [r_ref[0, :, sl] for sl in lanes_of]
    k = [k_ref[0, :, sl] for sl in lanes_of]
    v = [v_ref[0, :, sl] for sl in lanes_of]
    ld = [ld_ref[0, :, sl] for sl in lanes_of]
    a = [a_ref[0, :, sl] for sl in lanes_of]
    kk_raw = each(lambda x, sl: x * kkw_ref[:, sl], k, lanes_of)
    segsum = lambda x: _dot(bf(x), ones_bd)
    kk_norm = each(lambda x: jnp.sqrt(segsum(x * x)), kk_raw)
    kk = each(lambda x, n: x / jnp.maximum(n, 1e-12), kk_raw, kk_norm)
    kp = each(lambda x, y, sl: x * (1.0 + (y - 1.0) * kaw_ref[:, sl]), k, a, lanes_of)
    ka = each(lambda x, y: x * y, kk, a)
    ld_pieces = each(lambda x: _split3(jnp.concatenate([x, jnp.zeros_like(x)], axis=0))[:2], ld)
    cum2 = [_dot(ltri, hi) + _dot(ltri, lo) for hi, lo in ld_pieces]
    cum = [x[:c] for x in cum2]
    gam_end = [jnp.exp(x[c:]) for x in cum2]
    gam_inv = [jnp.exp(-x) for x in cum]
    at = each(lambda x, y: x * y, ka, gam_inv)
    kt = each(lambda x, y: x * y, kp, gam_inv)
    x_st = each(lambda kk_, r_, cum_, ld_: bf(jnp.concatenate(
        [stack(kk_ * jnp.exp(cum_ - ld_)), stack(r_ * jnp.exp(cum_))], axis=0)), kk, r, cum, ld)
    y_st = each(lambda x, y: bf(jnp.concatenate([stack(x), stack(y)], axis=0)), at, kt)
    gram = each(_dot_nt, x_st, y_st)
    m_a = [jnp.where(strict, x[:LANES, :LANES], 0.0) for x in gram]
    m_k = [bf(jnp.where(strict, x[:LANES, LANES:], 0.0)) for x in gram]
    n_a = [bf(jnp.where(incl, x[LANES:, :LANES], 0.0)) for x in gram]
    n_k = [bf(jnp.where(incl, x[LANES:, LANES:], 0.0)) for x in gram]
    power = [-x for x in m_a]
    inv = [eye + x for x in power]
    for _ in range(5):
        power = [_dot(bf(x), bf(x)) for x in power]
        inv = each(lambda t, p: t + _dot(bf(t), bf(p)), inv, power)
    inv = [bf(x) for x in inv]
    v_st = [stack(x) for x in v]
    v_stb = [bf(x) for x in v_st]
    v_stt = [bf(x.T) for x in v_st]
    at_end = each(lambda x, y: bf(stack(x * y)), at, gam_end)
    kt_end = each(lambda x, y: bf(stack(x * y)), kt, gam_end)
    s_from_v = each(_dot, v_stt, kt_end)
    m_kv = each(_dot, m_k, v_stb)
    n_kv = each(_dot, n_k, v_stb)
    s = [s_scr[pp] for pp in pairs]
    proj = each(lambda x, s_: _dot_nt(x, bf(s_)), x_st, s)
    u = each(lambda t, p, mv: -_dot(t, bf(p[:LANES] + mv)), inv, proj, m_kv)
    y2 = each(lambda p, n, u_, nv: p[LANES:] + _dot(n, bf(u_)) + nv, proj, n_a, u, n_kv)
    s_from_u = each(lambda u_, x: _dot(bf(u_.T), x), u, at_end)
    for pp in pairs:
        s_scr[pp] = s[pp] * jnp.concatenate([gam_end[pp], gam_end[pp]], axis=0) + s_from_u[pp] + s_from_v[pp]
    y = [x[:c] + x[c:] for x in y2]
    mean = [segsum(x) * (1.0 / RWKV_HEAD) for x in y]
    yc = each(lambda x, m: x - m, y, mean)
    var = [segsum(x * x) * (1.0 / RWKV_HEAD) for x in yc]
    bonus_dot = each(lambda r_, kp_, sl: segsum(r_ * kp_ * rkw_ref[:, sl]), r, kp, lanes_of)
    for pp in pairs:
        sl = lanes_of[pp]
        yn = yc[pp] * lax.rsqrt(var[pp] + GN_EPS) * lnw_ref[:, sl] + lnb_ref[:, sl]
        o_ref[0, :, sl] = (yn + bonus_dot[pp] * v[pp]) * g_ref[0, :, sl]

    @pl.when(t_chunk == pl.num_programs(2) - 1)
    def _store_state():
        for pp in range(WKV_STEP_PAIRS):
            s = s_scr[pp]
            st_ref[0, 2 * pp] = s[:RWKV_HEAD, :RWKV_HEAD]
            st_ref[0, 2 * pp + 1] = s[RWKV_HEAD:, RWKV_HEAD:]


def _wkv_chunked(rkv, logd, a, g, kkw, kaw, rkw, lnw, lnb, batch, seq_len):
    d = D_MODEL
    c = WKV_CHUNK
    w = WKV_STEP_LANES
    rkv4 = rkv.reshape(3, batch, seq_len, d)
    seq3 = lambda x: x.reshape(batch, seq_len, d)
    tok = lambda which: pl.BlockSpec((None, 1, c, w), lambda b, p, t, which=which: (which, b, t, p))
    tok3 = pl.BlockSpec((1, c, w), lambda b, p, t: (b, t, p))
    vec = pl.BlockSpec((1, w), lambda b, p, t: (0, p))
    state = pl.BlockSpec((1, 2 * WKV_STEP_PAIRS, RWKV_HEAD, RWKV_HEAD), lambda b, p, t: (b, p, 0, 0))
    o, st = pl.pallas_call(
        _wkv_chunk_body,
        grid=(batch, d // w, seq_len // c),
        in_specs=[tok(0), tok(1), tok(2), tok3, tok3, tok3, vec, vec, vec, vec, vec],
        out_specs=[tok3, state],
        out_shape=[jax.ShapeDtypeStruct((batch, seq_len, d), F32),
                   jax.ShapeDtypeStruct((batch, RWKV_HEADS, RWKV_HEAD, RWKV_HEAD), F32)],
        scratch_shapes=[pltpu.VMEM((WKV_STEP_PAIRS, LANES, LANES), F32)],
        compiler_params=_params("parallel", "parallel", "arbitrary"),
        name="wkv_chunked",
    )(rkv4, rkv4, rkv4, seq3(logd), seq3(a), seq3(g), kkw, kaw, rkw, lnw, lnb)
    return o.reshape(batch * seq_len, d), st


SB_STEP_HEADS = 8
SB_STEP_LANES = SB_STEP_HEADS * SB_HEAD_DIM


def _sb_prompt_body(q_ref, k_ref, v_ref, b_ref, o_ref, acc_scr, tail_scr):
    qi = pl.program_id(2)
    scale = SB_HEAD_DIM ** -0.5
    t_idx = lax.broadcasted_iota(jnp.int32, (Q_BLOCK, Q_BLOCK), 0)
    s_idx = lax.broadcasted_iota(jnp.int32, (Q_BLOCK, Q_BLOCK), 1)
    later = jnp.where(t_idx > s_idx, 1.0, 0.0).astype(BF16)
    later_and_ones = jnp.concatenate([later, jnp.ones((Q_BLOCK, Q_BLOCK), BF16)], axis=1)

    heads = range(SB_STEP_HEADS)
    lanes_of = [slice(hh * SB_HEAD_DIM, (hh + 1) * SB_HEAD_DIM) for hh in heads]

    def visit(kb, diagonal):
        start = pl.multiple_of(kb * Q_BLOCK, Q_BLOCK)
        keep = (lambda x: jnp.where(s_idx < t_idx, x, 0.0)) if diagonal else (lambda x: x)
        z = [_dot_nt(q_ref[0, :, sl].astype(BF16), k_ref[0, pl.ds(start, Q_BLOCK), sl].astype(BF16)) * scale
             + b_ref[:, sl] for sl in lanes_of]
        sp = [_softplus(x) for x in z]
        log_not = [keep(-x).astype(BF16) for x in sp]
        sums = [_dot(x, later_and_ones) for x in log_not]
        tail = [0.0 if diagonal else tail_scr[hh] for hh in heads]
        p = [keep(jnp.exp(z[hh] - sp[hh] + sums[hh][:, :Q_BLOCK] + tail[hh])).astype(BF16) for hh in heads]
        pv = [_dot(p[hh], v_ref[0, pl.ds(start, Q_BLOCK), lanes_of[hh]].astype(BF16)) for hh in heads]
        for hh in heads:
            acc_scr[hh] = pv[hh] if diagonal else acc_scr[hh] + pv[hh]
            tail_scr[hh] = tail[hh] + sums[hh][:, Q_BLOCK:]

    visit(qi, True)

    def body(j, carry):
        visit(qi - j, False)
        return carry

    lax.fori_loop(1, qi + 1, body, 0)
    for hh in heads:
        o_ref[0, :, hh * SB_HEAD_DIM:(hh + 1) * SB_HEAD_DIM] = acc_scr[hh]


def _sb_prompt(q, k, v, bias_cols, batch, seq_len):
    d = D_MODEL
    w = SB_STEP_LANES
    seq3 = lambda x: x.reshape(batch, seq_len, d)
    qspec = pl.BlockSpec((1, Q_BLOCK, w), lambda b, h, i: (b, i, h))
    kspec = pl.BlockSpec((1, seq_len, w), lambda b, h, i: (b, 0, h))
    vspec = kspec
    block_state = pltpu.VMEM((SB_STEP_HEADS, Q_BLOCK, SB_HEAD_DIM), F32)
    out = pl.pallas_call(
        _sb_prompt_body,
        grid=(batch, d // w, seq_len // Q_BLOCK),
        in_specs=[qspec, kspec, vspec, pl.BlockSpec((1, w), lambda b, h, i: (0, h))],
        out_specs=pl.BlockSpec((1, Q_BLOCK, w), lambda b, h, i: (b, i, h)),
        out_shape=jax.ShapeDtypeStruct((batch, seq_len, d), F32),
        scratch_shapes=[block_state, block_state],
        compiler_params=_params("parallel", "parallel", "arbitrary"),
        name="sb_prompt",
    )(seq3(q), seq3(k), seq3(v), bias_cols)
    return out.reshape(batch * seq_len, d)


PAGE_ROWS = PAGE_SIZE * SB_HEADS


SB_STEP_SEQS = 4


def _sb_sample_body(pt_ref, q_ref, kn_ref, vn_ref, *refs):
    n = SB_STEP_SEQS
    kc_refs, vc_refs = refs[:n], refs[n:2 * n]
    b_ref, col_ref, exp_ref, o_ref, acc_scr, tail_scr = refs[2 * n:]
    seqs = range(n)
    step = pl.program_id(1)
    n_steps = pl.num_programs(1)
    scale = SB_HEAD_DIM ** -0.5
    bias = b_ref[...]
    head = lax.broadcasted_iota(jnp.int32, (SB_HEADS, PAGE_ROWS), 0)
    own_head = lax.broadcasted_iota(jnp.int32, (SB_HEADS, PAGE_ROWS), 1) % SB_HEADS == head
    r_idx = lax.broadcasted_iota(jnp.int32, (PAGE_SIZE, PAGE_SIZE), 0)
    c_idx = lax.broadcasted_iota(jnp.int32, (PAGE_SIZE, PAGE_SIZE), 1)
    later = jnp.where(r_idx > c_idx, 1.0, 0.0).astype(BF16)
    ones = jnp.ones((PAGE_SIZE, PAGE_SIZE), BF16)

    @pl.when(step == 0)
    def _init():
        n_past = n_steps * PAGE_SIZE
        new_is_before = jnp.full((SB_HEADS, LANES), n_past, jnp.int32) < n_past
        for i in seqs:
            z_new = jnp.sum(q_ref[i] * kn_ref[i], axis=-1, keepdims=True) * scale + bias
            sp_new = _softplus(z_new)
            tail_scr[i] = jnp.where(new_is_before, -sp_new, 0.0)
            acc_scr[i] = jnp.where(new_is_before, jnp.exp(z_new - sp_new), 0.0) * vn_ref[i]

    z_all = [_dot_nt(q_ref[i].astype(BF16), kc_refs[i][0].astype(BF16)) for i in seqs]
    z_own = [_split3(jnp.where(own_head, x, 0.0)) for x in z_all]
    collapse = col_ref[...]
    z = [(_dot(p3[0], collapse) + _dot(p3[1], collapse) + _dot(p3[2], collapse)) * scale + bias for p3 in z_own]
    sp = [_softplus(x) for x in z]
    log_not = [(-x).astype(BF16) for x in sp]
    after = [_dot(x, later) for x in log_not]
    total = [_dot(x, ones) for x in log_not]
    tail = [tail_scr[i] for i in seqs]
    p = [jnp.exp(z[i] - sp[i] + after[i] + tail[i]).astype(BF16) for i in seqs]
    p_rows = [jnp.where(own_head, _dot(x, exp_ref[...]), 0.0).astype(BF16) for x in p]
    pv = [_dot(p_rows[i], vc_refs[i][0].astype(BF16)) for i in seqs]
    for i in seqs:
        acc_scr[i] += pv[i]
        tail_scr[i] = tail[i] + total[i]

    @pl.when(step == n_steps - 1)
    def _finish():
        o_ref[...] = acc_scr[...]


def _sb_sample(q, k_new, v_new, cache_k, cache_v, page_table, bias):
    batch, d = q.shape
    n_pages = page_table.shape[1]
    n_phys = cache_k.shape[0]
    pages = lambda c: c.reshape(n_phys, PAGE_ROWS, SB_HEAD_DIM)
    row_token = lax.broadcasted_iota(jnp.int32, (PAGE_ROWS, PAGE_SIZE), 0) // SB_HEADS
    collapse = (row_token == lax.broadcasted_iota(jnp.int32, (PAGE_ROWS, PAGE_SIZE), 1)).astype(BF16)
    bias_lanes = jnp.broadcast_to(bias[:, None], (SB_HEADS, LANES)).astype(F32)
    n = SB_STEP_SEQS
    tok = pl.BlockSpec((n, SB_HEADS, SB_HEAD_DIM), lambda b, s, pt: (b, 0, 0))
    page = lambda i: pl.BlockSpec((1, PAGE_ROWS, SB_HEAD_DIM),
                                  lambda b, s, pt, i=i: (pt[b * n + i, n_pages - 1 - s], 0, 0))
    const = lambda shape: pl.BlockSpec(shape, lambda b, s, pt: (0, 0))
    heads3 = lambda x: x.reshape(batch, SB_HEADS, SB_HEAD_DIM)
    per_seq = pltpu.VMEM((n, SB_HEADS, SB_HEAD_DIM), F32)
    out = pl.pallas_call(
        _sb_sample_body,
        grid_spec=pltpu.PrefetchScalarGridSpec(
            num_scalar_prefetch=1,
            grid=(batch // n, n_pages),
            in_specs=[tok, tok, tok] + [page(i) for i in range(n)] * 2 + [
                const((SB_HEADS, LANES)), const((PAGE_ROWS, PAGE_SIZE)), const((PAGE_SIZE, PAGE_ROWS))],
            out_specs=tok,
            scratch_shapes=[per_seq, per_seq],
        ),
        out_shape=jax.ShapeDtypeStruct((batch, SB_HEADS, SB_HEAD_DIM), F32),
        compiler_params=_params("parallel", "arbitrary"),
        name="sb_sample",
    )(page_table, heads3(q), heads3(k_new), heads3(v_new), *([pages(cache_k)] * n), *([pages(cache_v)] * n),
      bias_lanes, collapse, collapse.T)
    return out.reshape(batch, d)


ROUTER_OFFSET = N_GROUPS


def _router_body(x_ref, w_ref, b_ref, gate_ref, route_ref, count_ref, count_scr):
    logits = jnp.dot(x_ref[...], w_ref[...], preferred_element_type=F32,
                     precision=lax.Precision.HIGHEST) + b_ref[...]
    lane = lax.broadcasted_iota(jnp.int32, logits.shape, 1)
    neg = -jnp.inf
    first_lane_of = lambda hit: jnp.min(jnp.where(hit, lane, LANES), axis=-1, keepdims=True)
    is_group = lane < N_GROUPS
    glog = jnp.where(is_group, logits, neg)
    gmax = jnp.max(glog, axis=-1, keepdims=True)
    g_sel = first_lane_of(glog == gmax)
    p_group = 1.0 / jnp.sum(jnp.exp(glog - gmax), axis=-1, keepdims=True)
    in_group = ((lane - ROUTER_OFFSET) // EXPERTS_PER_GROUP == g_sel) & (lane >= ROUTER_OFFSET) \
        & (lane < ROUTER_OFFSET + N_EXPERTS)
    elog = jnp.where(in_group, logits, neg)
    top1 = jnp.max(elog, axis=-1, keepdims=True)
    lane1 = first_lane_of(elog == top1)
    rest = jnp.where(lane == lane1, neg, elog)
    top2 = jnp.max(rest, axis=-1, keepdims=True)
    lane2 = first_lane_of(rest == top2)
    e2 = jnp.exp(top2 - top1)
    w1 = p_group / (1.0 + e2)
    w2 = p_group * e2 / (1.0 + e2)
    gate_ref[...] = jnp.where(lane == lane1, w1, 0.0) + jnp.where(lane == lane2, w2, 0.0)
    @pl.when(pl.program_id(0) == 0)
    def _zero_counts():
        count_scr[...] = jnp.zeros_like(count_scr)

    tm = logits.shape[0]
    earlier = (lax.broadcasted_iota(jnp.int32, (tm, tm), 1)
               < lax.broadcasted_iota(jnp.int32, (tm, tm), 0)).astype(BF16)
    hot1 = jnp.where(lane == lane1, 1.0, 0.0)
    hot2 = jnp.where(lane == lane2, 1.0, 0.0)
    seen = count_scr[0:1, :]
    total1 = jnp.sum(hot1, axis=0, keepdims=True)
    rank1 = jnp.sum(hot1 * (seen + _dot(earlier, hot1.astype(BF16))), axis=-1, keepdims=True)
    rank2 = jnp.sum(hot2 * (seen + total1 + _dot(earlier, hot2.astype(BF16))), axis=-1, keepdims=True)
    seen = seen + total1 + jnp.sum(hot2, axis=0, keepdims=True)
    count_scr[...] = jnp.broadcast_to(seen, count_scr.shape)
    count_ref[...] = jnp.broadcast_to(seen, count_ref.shape)
    route_ref[...] = (jnp.where(lane == 0, (lane1 - ROUTER_OFFSET).astype(F32), 0.0)
                      + jnp.where(lane == 1, (lane2 - ROUTER_OFFSET).astype(F32), 0.0)
                      + jnp.where(lane == 2, w1, 0.0) + jnp.where(lane == 3, w2, 0.0)
                      + jnp.where(lane == 4, rank1, 0.0) + jnp.where(lane == 5, rank2, 0.0))


def _router(x, w_router, b_router, tm):
    n, d = x.shape
    out = jax.ShapeDtypeStruct((n, LANES), F32)
    row = pl.BlockSpec((tm, LANES), lambda i: (i, 0))
    counts = pl.BlockSpec((SUBLANES, LANES), lambda i: (0, 0))
    return pl.pallas_call(
        _router_body,
        grid=(n // tm,),
        in_specs=[pl.BlockSpec((tm, d), lambda i: (i, 0)),
                  pl.BlockSpec((d, LANES), lambda i: (0, 0)),
                  pl.BlockSpec((1, LANES), lambda i: (0, 0))],
        out_specs=[row, row, counts],
        out_shape=[out, out, jax.ShapeDtypeStruct((SUBLANES, LANES), F32)],
        scratch_shapes=[pltpu.VMEM((SUBLANES, LANES), F32)],
        compiler_params=_params("arbitrary"),
        name="moe_router",
    )(x, w_router, b_router)


def _moe_body(x_ref, gate_ref, res_ref, wg_ref, wu_ref, wd_ref, o_ref, xb_scr):
    e = pl.program_id(1)

    @pl.when(e == 0)
    def _init():
        xb_scr[...] = x_ref[...].astype(BF16)
        o_ref[...] = res_ref[...]

    xb = xb_scr[...]
    hg = _dot(xb, wg_ref[0].astype(BF16))
    hu = _dot(xb, wu_ref[0].astype(BF16))
    gate = gate_ref[...]
    lane = lax.broadcasted_iota(jnp.int32, gate.shape, 1)
    ge = jnp.sum(jnp.where(lane == e + ROUTER_OFFSET, gate, 0.0), axis=-1, keepdims=True)
    act = (hg * _sigmoid(hg)) * hu * ge
    o_ref[...] += _dot(act.astype(BF16), wd_ref[0].astype(BF16))


def _moe_dense(x, gate, residual, wg, wu, wd, layer, tm):
    n, d = x.shape
    row = lambda i, e: (i, 0)
    return pl.pallas_call(
        _moe_body,
        grid=(n // tm, N_EXPERTS),
        in_specs=[pl.BlockSpec((tm, d), row), pl.BlockSpec((tm, LANES), row), pl.BlockSpec((tm, d), row),
                  pl.BlockSpec((None, 1, d, D_EXPERT), lambda i, e: (layer, e, 0, 0)),
                  pl.BlockSpec((None, 1, d, D_EXPERT), lambda i, e: (layer, e, 0, 0)),
                  pl.BlockSpec((None, 1, D_EXPERT, d), lambda i, e: (layer, e, 0, 0))],
        out_specs=pl.BlockSpec((tm, d), row),
        out_shape=jax.ShapeDtypeStruct((n, d), F32),
        scratch_shapes=[pltpu.VMEM((tm, d), BF16)],
        compiler_params=_params("parallel", "arbitrary"),
        name="moe_experts",
    )(x, gate, residual, wg, wu, wd)


MOE_TILE = 256
COMBINE_TILE = 256


def _row_copy(src_hbm, index_ref, base, buf, sem):
    def copy(r):
        return pltpu.make_async_copy(src_hbm.at[pl.ds(index_ref[base + r], 1)], buf.at[pl.ds(r, 1)], sem)
    return copy


def _for_rows(n_rows, fn):
    def body(r, carry):
        fn(r)
        return carry
    lax.fori_loop(0, n_rows, body, 0, unroll=8)


def _gather_pipeline(step, n_steps, copies_of, n_rows):
    slot = step % 2

    @pl.when(step == 0)
    def _prime():
        _for_rows(n_rows, lambda r: copies_of(0, 0)(r).start())

    @pl.when(step + 1 < n_steps)
    def _prefetch():
        _for_rows(n_rows, lambda r: copies_of(step + 1, 1 - slot)(r).start())

    _for_rows(n_rows, lambda r: copies_of(step, slot)(r).wait())
    return slot


def _moe_dispatch_body(pos_ref, h_ref, zeros_hbm, xs_hbm, sem):
    del zeros_hbm
    i = pl.program_id(0)
    tm = h_ref.shape[0]

    def copy(j):
        r = j % tm
        return pltpu.make_async_copy(h_ref.at[pl.ds(r, 1)], xs_hbm.at[pl.ds(pos_ref[i * 2 * tm + j], 1)], sem)

    _for_rows(2 * tm, lambda j: copy(j).start())
    _for_rows(2 * tm, lambda j: copy(j).wait())


def _moe_sparse_body(tile_expert_ref, n_used_ref, x_ref, wg_ref, wu_ref, wd_ref, y_ref, wg_bf, wu_bf, wd_bf):
    t = pl.program_id(0)

    @pl.when(jnp.logical_or(t == 0, tile_expert_ref[t] != tile_expert_ref[jnp.maximum(t - 1, 0)]))
    def _new_expert():
        wg_bf[...] = wg_ref[0].astype(BF16)
        wu_bf[...] = wu_ref[0].astype(BF16)
        wd_bf[...] = wd_ref[0].astype(BF16)

    @pl.when(t < n_used_ref[0])
    def _compute():
        x = x_ref[...].astype(BF16)
        hg = _dot(x, wg_bf[...])
        hu = _dot(x, wu_bf[...])
        y_ref[...] = _dot(((hg * _sigmoid(hg)) * hu).astype(BF16), wd_bf[...])

    @pl.when(t >= n_used_ref[0])
    def _unused_tile():
        y_ref[...] = jnp.zeros_like(y_ref)


def _moe_combine_body(pos_ref, x_ref, route_ref, y_hbm, *rest, normed):
    g_ref = rest[0] if normed else None
    o_ref, ybuf, sem = rest[-3:]
    i = pl.program_id(0)
    tm = x_ref.shape[0]
    copies_of = lambda tile, s: _row_copy(y_hbm, pos_ref, tile * 2 * tm, ybuf.at[s], sem.at[s])
    slot = _gather_pipeline(i, pl.num_programs(0), copies_of, 2 * tm)
    rows = ybuf[slot]
    route = route_ref[...]
    x = x_ref[...] + route[:, 2:3] * rows[:tm] + route[:, 3:4] * rows[tm:]
    if normed:
        x = x * lax.rsqrt(jnp.mean(x * x, axis=-1, keepdims=True) + RMS_EPS) * g_ref[...]
    o_ref[...] = x


def _route_plan(route, counts, tile, tm):
    n = route.shape[0]
    counts = counts[0, ROUTER_OFFSET:ROUTER_OFFSET + N_EXPERTS].astype(jnp.int32)
    padded = (counts + tile - 1) // tile * tile
    ends = jnp.cumsum(padded)
    offs = (ends - padded).astype(F32)
    expert = route[:, 0:2]
    hot = expert[:, :, None] == jnp.arange(N_EXPERTS, dtype=F32)[None, None, :]
    pos = (jnp.sum(jnp.where(hot, offs[None, None, :], 0.0), axis=-1) + route[:, 4:6]).astype(jnp.int32)
    n_tiles = (2 * n + N_EXPERTS * tile) // tile
    tile_start = jnp.arange(n_tiles, dtype=jnp.int32) * tile
    tile_expert = jnp.minimum(jnp.sum((ends[None, :] <= tile_start[:, None]).astype(jnp.int32), axis=1),
                              N_EXPERTS - 1)
    n_used = (ends[-1] // tile).astype(jnp.int32).reshape(1)
    pos_tiled = pos.reshape(n // tm, tm, 2).transpose(0, 2, 1).reshape(2 * n)
    return pos_tiled, tile_expert, n_used


def _moe_sparse(h, route, counts, residual, wg, wu, wd, layer, out_norm=None):
    n, d = h.shape
    tile, tm = MOE_TILE, COMBINE_TILE
    pos_tiled, tile_expert, n_used = _route_plan(route, counts, tile, tm)
    p_rows = 2 * n + N_EXPERTS * tile
    token_tile = pl.BlockSpec((tm, d), lambda i, pos: (i, 0))
    xs = pl.pallas_call(
        _moe_dispatch_body,
        grid_spec=pltpu.PrefetchScalarGridSpec(
            num_scalar_prefetch=1,
            grid=(n // tm,),
            in_specs=[token_tile, pl.BlockSpec(memory_space=pl.ANY)],
            out_specs=pl.BlockSpec(memory_space=pl.ANY),
            scratch_shapes=[pltpu.SemaphoreType.DMA(())],
        ),
        out_shape=jax.ShapeDtypeStruct((p_rows, d), F32),
        input_output_aliases={2: 0},
        compiler_params=_params("arbitrary"),
        name="moe_dispatch",
    )(pos_tiled, h, jnp.zeros((p_rows, d), F32))
    y = pl.pallas_call(
        _moe_sparse_body,
        grid_spec=pltpu.PrefetchScalarGridSpec(
            num_scalar_prefetch=2,
            grid=(p_rows // tile,),
            in_specs=[pl.BlockSpec((tile, d), lambda t, te, nu: (t, 0)),
                      pl.BlockSpec((None, 1, d, D_EXPERT), lambda t, te, nu: (layer, te[t], 0, 0)),
                      pl.BlockSpec((None, 1, d, D_EXPERT), lambda t, te, nu: (layer, te[t], 0, 0)),
                      pl.BlockSpec((None, 1, D_EXPERT, d), lambda t, te, nu: (layer, te[t], 0, 0))],
            out_specs=pl.BlockSpec((tile, d), lambda t, te, nu: (t, 0)),
            scratch_shapes=[pltpu.VMEM((d, D_EXPERT), BF16), pltpu.VMEM((d, D_EXPERT), BF16),
                            pltpu.VMEM((D_EXPERT, d), BF16)],
        ),
        out_shape=jax.ShapeDtypeStruct((p_rows, d), F32),
        compiler_params=_params("arbitrary"),
        name="moe_sparse_experts",
    )(tile_expert, n_used, xs, wg, wu, wd)
    normed = out_norm is not None
    norm_specs = [pl.BlockSpec((1, d), lambda i, pos: (0, 0))] if normed else []
    norm_args = [out_norm.reshape(1, d)] if normed else []
    return pl.pallas_call(
        functools.partial(_moe_combine_body, normed=normed),
        grid_spec=pltpu.PrefetchScalarGridSpec(
            num_scalar_prefetch=1,
            grid=(n // tm,),
            in_specs=[token_tile, pl.BlockSpec((tm, LANES), lambda i, pos: (i, 0)),
                      pl.BlockSpec(memory_space=pl.ANY)] + norm_specs,
            out_specs=token_tile,
            scratch_shapes=[pltpu.VMEM((2, 2 * tm, d), F32), pltpu.SemaphoreType.DMA((2,))],
        ),
        out_shape=jax.ShapeDtypeStruct((n, d), F32),
        compiler_params=_params("arbitrary"),
        name="moe_combine",
    )(pos_tiled, residual, route, y, *norm_args)


def _row_tile(n):
    return 256 if n % 256 == 0 else LANES


def _ffn(x, norm_g, w_router, b_router, wg, wu, wd, layer, out_norm=None):
    n = x.shape[0]
    tm = _row_tile(n)
    h = _rmsnorm(x, norm_g, tm)
    gate, route, counts = _router(h, w_router, b_router, tm)
    if n % COMBINE_TILE == 0 and n >= N_EXPERTS * MOE_TILE:
        return _moe_sparse(h, route, counts, x, wg, wu, wd, layer, out_norm)
    out = _moe_dense(h, gate, x, wg, wu, wd, layer, LANES)
    return out if out_norm is None else _rmsnorm(out, out_norm, tm)


def kernel(x_prompt, x_sample, state_tshift, state_wkv, cache_k, cache_v, page_table, norm_mix, norm_ffn, norm_final, rwkv_mu, rwkv_w0, rwkv_w1, rwkv_w2, rwkv_a0, rwkv_a1, rwkv_a2, rwkv_g1, rwkv_g2, rwkv_kk, rwkv_ka, rwkv_rk, rwkv_wr, rwkv_wk, rwkv_wv, rwkv_wo, rwkv_lnw, rwkv_lnb, sb_wqkv, sb_wo, sb_bias, moe_wgroup, moe_bgroup, moe_wexpert, moe_bexpert, moe_wgate, moe_wup, moe_wdown):
    d = D_MODEL
    bp, tp, _ = x_prompt.shape
    bs, ts, _ = x_sample.shape
    xp = x_prompt.reshape(bp * tp, d)
    xs = x_sample.reshape(bs * ts, d)
    row1 = lambda v: v.reshape(1, d)
    pad_cols = lambda w: jnp.pad(w, ((0, 0), (0, LORA_PAD - w.shape[1]))).astype(BF16)
    pad_rows = lambda w: jnp.pad(w, ((0, LORA_PAD - w.shape[0]), (0, 0))).astype(BF16)

    mu = rwkv_mu[0]
    mu_rkv = jnp.stack([mu[0], mu[2], mu[3]]).reshape(3, 1, d)
    mu_lora = jnp.stack([mu[1], mu[4], mu[5]]).reshape(3, 1, d)
    w_rkv = jnp.stack([rwkv_wr[0], rwkv_wk[0], rwkv_wv[0]]).astype(BF16)
    lora = (row1(rwkv_w0[0]), row1(rwkv_a0[0]), pad_cols(rwkv_w1[0]), pad_rows(rwkv_w2[0]),
            pad_cols(rwkv_a1[0]), pad_rows(rwkv_a2[0]), rwkv_g1[0].astype(BF16), rwkv_g2[0].astype(BF16))
    head_vecs = (row1(rwkv_kk[0]), row1(rwkv_ka[0]), row1(rwkv_rk[0]), row1(rwkv_lnw[0]), row1(rwkv_lnb[0]))
    wo = rwkv_wo[0].astype(BF16)

    def rwkv_layer(x, prev, s0, batch, seq_len):
        tm = _row_tile(x.shape[0])
        h = _rmsnorm(x, norm_mix[0], tm)
        rkv = _rwkv_rkv(h, prev, mu_rkv, w_rkv, tm, seq_len)
        logd, a, g = _rwkv_lora(h, prev, mu_lora, *lora, tm, seq_len)
        if s0 is None:
            o, s_final = _wkv_chunked(rkv, logd, a, g, *head_vecs, batch, seq_len)
        else:
            assert seq_len == 1, "a given initial state is only supported for single-token sequences"
            o, s_final = _wkv_step(rkv, logd, a, g, *head_vecs, s0)
        return _matmul(o, wo, tm, residual=x), h, s_final

    xp, hp, wkv_p = rwkv_layer(xp, None, None, bp, tp)
    xs, hs, wkv_s = rwkv_layer(xs, state_tshift[0], state_wkv[0], bs, ts)
    shift_p = hp.reshape(bp, tp, d)[:, -1]
    shift_s = hs.reshape(bs, ts, d)[:, -1]

    def ffn(x, i, out_norm=None):
        w_router = jnp.pad(jnp.concatenate([moe_wgroup[i], moe_wexpert[i]], axis=1),
                           ((0, 0), (0, LANES - N_GROUPS - N_EXPERTS)))
        b_router = jnp.pad(jnp.concatenate([moe_bgroup[i], moe_bexpert[i]]),
                           (0, LANES - N_GROUPS - N_EXPERTS)).reshape(1, LANES)
        return _ffn(x, norm_ffn[i], w_router, b_router, moe_wgate, moe_wup, moe_wdown, i, out_norm)

    xp = ffn(xp, 0)
    xs = ffn(xs, 0)

    w_qkv = jnp.stack(jnp.split(sb_wqkv[0], 3, axis=-1)).astype(BF16)
    w_att_out = sb_wo[0].astype(BF16)
    bias_cols = jnp.repeat(sb_bias[0].astype(F32), SB_HEAD_DIM).reshape(1, d)

    def qkv_proj(x):
        tm = _row_tile(x.shape[0])
        h = _rmsnorm(x, norm_mix[1], tm)
        return [_matmul(h, w_qkv[j], tm) for j in range(3)], tm

    qkv_p, tm_p = qkv_proj(xp)
    att_p = _sb_prompt(*qkv_p, bias_cols, bp, tp)
    xp = _matmul(att_p, w_att_out, tm_p, residual=xp)

    qkv_s, tm_s = qkv_proj(xs)
    att_s = _sb_sample(qkv_s[0], qkv_s[1], qkv_s[2], cache_k[0], cache_v[0], page_table, sb_bias[0])
    xs = _matmul(att_s, w_att_out, tm_s, residual=xs)

    y_prompt = ffn(xp, 1, norm_final).reshape(bp, tp, d)
    y_sample = ffn(xs, 1, norm_final).reshape(bs, ts, d)
    heads5 = lambda x, b, t: x.reshape(1, b, t, SB_HEADS, SB_HEAD_DIM)
    return (y_prompt, y_sample,
            shift_p[None], wkv_p[None],
            heads5(qkv_p[1], bp, tp), heads5(qkv_p[2], bp, tp),
            shift_s[None], wkv_s[None],
            heads5(qkv_s[1], bs, ts), heads5(qkv_s[2], bs, ts))
```

```python
import functools

import jax
import jax.numpy as jnp
from jax import lax
from jax.experimental import pallas as pl
from jax.experimental.pallas import tpu as pltpu

F32 = jnp.float32
BF16 = jnp.bfloat16

D_MODEL = 2048
RWKV_HEAD = 64
RWKV_HEADS = D_MODEL // RWKV_HEAD
HEAD_PAIRS = RWKV_HEADS // 2
LORA_PAD = 128
GN_EPS = 64e-5
RMS_EPS = 1e-6
SB_HEADS = 16
SB_HEAD_DIM = D_MODEL // SB_HEADS
Q_BLOCK = 128
PAGE_SIZE = 128
N_GROUPS = 4
EXPERTS_PER_GROUP = 8
N_EXPERTS = N_GROUPS * EXPERTS_PER_GROUP
D_EXPERT = 256
LANES = 128
SUBLANES = 8
VMEM_LIMIT_BYTES = 52 * 1024 * 1024


def _params(*semantics):
    return pltpu.CompilerParams(dimension_semantics=semantics, vmem_limit_bytes=VMEM_LIMIT_BYTES)


def _softplus(x):
    return jnp.maximum(x, 0.0) + jnp.log(1.0 + jnp.exp(-jnp.abs(x)))


def _sigmoid(x):
    return 1.0 / (1.0 + jnp.exp(-x))


def _dot(a, b):
    return jnp.dot(a, b, preferred_element_type=F32)


def _dot_hilo(a, b_bf16):
    hi = a.astype(BF16)
    lo = (a - hi.astype(F32)).astype(BF16)
    return _dot(hi, b_bf16) + _dot(lo, b_bf16)


def _block_ones(n, seg):
    r = lax.broadcasted_iota(jnp.int32, (n, n), 0) // seg
    c = lax.broadcasted_iota(jnp.int32, (n, n), 1) // seg
    return jnp.where(r == c, 1.0, 0.0).astype(BF16)


def _rmsnorm_body(x_ref, g_ref, o_ref):
    x = x_ref[...]
    ms = jnp.mean(x * x, axis=-1, keepdims=True)
    o_ref[...] = x * lax.rsqrt(ms + RMS_EPS) * g_ref[...]


def _rmsnorm(x, g, tm):
    n, d = x.shape
    return pl.pallas_call(
        _rmsnorm_body,
        grid=(n // tm,),
        in_specs=[pl.BlockSpec((tm, d), lambda i: (i, 0)), pl.BlockSpec((1, d), lambda i: (0, 0))],
        out_specs=pl.BlockSpec((tm, d), lambda i: (i, 0)),
        out_shape=jax.ShapeDtypeStruct((n, d), F32),
        compiler_params=_params("parallel"),
        name="rmsnorm",
    )(x, g.reshape(1, d))


def _mm_body(x_ref, w_ref, o_ref):
    o_ref[...] = _dot(x_ref[...].astype(BF16), w_ref[...])


def _mm_res_body(x_ref, w_ref, r_ref, o_ref):
    o_ref[...] = r_ref[...] + _dot(x_ref[...].astype(BF16), w_ref[...])


def _matmul(x, w, tm, residual=None):
    n, k = x.shape
    m = w.shape[1]
    row = lambda i: (i, 0)
    in_specs = [pl.BlockSpec((tm, k), row), pl.BlockSpec((k, m), lambda i: (0, 0))]
    args = [x, w]
    body = _mm_body
    if residual is not None:
        in_specs.append(pl.BlockSpec((tm, m), row))
        args.append(residual)
        body = _mm_res_body
    return pl.pallas_call(
        body,
        grid=(n // tm,),
        in_specs=in_specs,
        out_specs=pl.BlockSpec((tm, m), row),
        out_shape=jax.ShapeDtypeStruct((n, m), F32),
        compiler_params=_params("parallel"),
        name="matmul",
    )(*args)


def _token_shift(h_ref, tail_ref, prev_ref, i, tm, seq_len):
    h = h_ref[...]
    if prev_ref is not None:
        return h, prev_ref[...]
    rolled = pltpu.roll(h, 1, 0)
    starts_sequence = (i * tm) % seq_len == 0
    first = jnp.where(starts_sequence, 0.0, tail_ref[SUBLANES - 1:SUBLANES, :])
    row = lax.broadcasted_iota(jnp.int32, h.shape, 0)
    return h, jnp.where(row == 0, first, rolled)


def _shift_specs(tm, d, has_prev, grid_rank):
    if grid_rank == 1:
        cur = lambda i: (i, 0)
        tail = lambda i: (jnp.maximum(i * (tm // SUBLANES) - 1, 0), 0)
    else:
        cur = lambda p, i: (i, 0)
        tail = lambda p, i: (jnp.maximum(i * (tm // SUBLANES) - 1, 0), 0)
    if has_prev:
        return [pl.BlockSpec((tm, d), cur), pl.BlockSpec((tm, d), cur)]
    return [pl.BlockSpec((tm, d), cur), pl.BlockSpec((SUBLANES, d), tail)]


def _rkv_body(h_ref, aux_ref, mu_ref, w_ref, o_ref, *, tm, seq_len, has_prev):
    i = pl.program_id(1)
    h, prev = _token_shift(h_ref, None if has_prev else aux_ref, aux_ref if has_prev else None, i, tm, seq_len)
    x = h + (prev - h) * mu_ref[0]
    o_ref[0] = _dot(x.astype(BF16), w_ref[0])


def _rwkv_rkv(h, prev, mu3, w3, tm, seq_len):
    n, d = h.shape
    has_prev = prev is not None
    return pl.pallas_call(
        functools.partial(_rkv_body, tm=tm, seq_len=seq_len, has_prev=has_prev),
        grid=(3, n // tm),
        in_specs=_shift_specs(tm, d, has_prev, 2) + [
            pl.BlockSpec((1, 1, d), lambda p, i: (p, 0, 0)),
            pl.BlockSpec((1, d, d), lambda p, i: (p, 0, 0)),
        ],
        out_specs=pl.BlockSpec((1, tm, d), lambda p, i: (p, i, 0)),
        out_shape=jax.ShapeDtypeStruct((3, n, d), F32),
        compiler_params=_params("arbitrary", "arbitrary"),
        name="rwkv_rkv",
    )(h, prev if has_prev else h, mu3, w3)


def _lora_body(h_ref, aux_ref, mu_ref, w0_ref, a0_ref, w1_ref, w2_ref, a1_ref, a2_ref, g1_ref, g2_ref,
               decay_ref, a_ref, g_ref, *, tm, seq_len, has_prev):
    i = pl.program_id(0)
    h, prev = _token_shift(h_ref, None if has_prev else aux_ref, aux_ref if has_prev else None, i, tm, seq_len)
    xx = prev - h
    xw = (h + xx * mu_ref[0]).astype(BF16)
    xa = (h + xx * mu_ref[1]).astype(BF16)
    xg = (h + xx * mu_ref[2]).astype(BF16)
    lw = _dot(jnp.tanh(_dot(xw, w1_ref[...])).astype(BF16), w2_ref[...])
    w = -_softplus(-(w0_ref[...] + lw)) - 0.5
    decay_ref[...] = -jnp.exp(w)
    la = _dot(_dot(xa, a1_ref[...]).astype(BF16), a2_ref[...])
    a_ref[...] = _sigmoid(a0_ref[...] + la)
    g_ref[...] = _dot(_sigmoid(_dot(xg, g1_ref[...])).astype(BF16), g2_ref[...])


def _rwkv_lora(h, prev, mu3, w0, a0, w1, w2, a1, a2, g1, g2, tm, seq_len):
    n, d = h.shape
    has_prev = prev is not None
    full = lambda arr: pl.BlockSpec(arr.shape, lambda i, nd=arr.ndim: (0,) * nd)
    row = pl.BlockSpec((tm, d), lambda i: (i, 0))
    out = jax.ShapeDtypeStruct((n, d), F32)
    return pl.pallas_call(
        functools.partial(_lora_body, tm=tm, seq_len=seq_len, has_prev=has_prev),
        grid=(n // tm,),
        in_specs=_shift_specs(tm, d, has_prev, 1) + [full(mu3), full(w0), full(a0), full(w1), full(w2),
                                                     full(a1), full(a2), full(g1), full(g2)],
        out_specs=[row, row, row],
        out_shape=[out, out, out],
        compiler_params=_params("parallel"),
        name="rwkv_lora",
    )(h, prev if has_prev else h, mu3, w0, a0, w1, w2, a1, a2, g1, g2)


STEP_SEQS = SUBLANES


def _segsum(x, ones_bd):
    return _dot_hilo(x, ones_bd)


def _wkv_step_body(r_ref, k_ref, v_ref, ld_ref, a_ref, g_ref, kkw_ref, kaw_ref, rkw_ref, lnw_ref, lnb_ref, s0_ref,
                   o_ref, st_ref):
    ones_bd = _block_ones(LANES, RWKV_HEAD)
    rows = lax.broadcasted_iota(jnp.int32, (RWKV_HEAD, LANES), 0)
    lanes = lax.broadcasted_iota(jnp.int32, (RWKV_HEAD, LANES), 1)
    diag = (lanes % RWKV_HEAD) == rows
    pairs = range(HEAD_PAIRS)
    lanes_of = [slice(p * LANES, (p + 1) * LANES) for p in pairs]
    r = [r_ref[:, sl] for sl in lanes_of]
    k = [k_ref[:, sl] for sl in lanes_of]
    v = [v_ref[:, sl] for sl in lanes_of]
    a = [a_ref[:, sl] for sl in lanes_of]
    d = [jnp.exp(ld_ref[:, sl]) for sl in lanes_of]
    kk_raw = [k[p] * kkw_ref[:, lanes_of[p]] for p in pairs]
    kk_norm = [jnp.sqrt(_segsum(x * x, ones_bd)) for x in kk_raw]
    kk = [kk_raw[p] / jnp.maximum(kk_norm[p], 1e-12) for p in pairs]
    kp = [k[p] * (1.0 + (a[p] - 1.0) * kaw_ref[:, lanes_of[p]]) for p in pairs]
    ka = [kk[p] * a[p] for p in pairs]
    dr = [d[p] * r[p] for p in pairs]
    c1 = [_segsum(ka[p] * r[p], ones_bd) for p in pairs]
    c2 = [_segsum(kp[p] * r[p], ones_bd) for p in pairs]
    y_rows = [[] for _ in pairs]
    for s in range(STEP_SEQS):
        row = lambda x: x[s:s + 1, :]
        state = [jnp.concatenate([s0_ref[s, 2 * p], s0_ref[s, 2 * p + 1]], axis=1) for p in pairs]
        s_kk = [_dot_hilo(state[p] * row(kk[p]), ones_bd) for p in pairs]
        y_part = [_dot_hilo(state[p] * row(dr[p]), ones_bd) for p in pairs]
        v_col = [_dot_hilo(jnp.where(diag, row(v[p]), 0.0), ones_bd) for p in pairs]
        for p in pairs:
            new = state[p] * row(d[p]) - s_kk[p] * row(ka[p]) + v_col[p] * row(kp[p])
            st_ref[s, 2 * p] = new[:, :RWKV_HEAD]
            st_ref[s, 2 * p + 1] = new[:, RWKV_HEAD:]
            y_col = y_part[p] - s_kk[p] * row(c1[p]) + v_col[p] * row(c2[p])
            y_rows[p].append(jnp.sum(jnp.where(diag, y_col, 0.0), axis=0, keepdims=True))
    y = [jnp.concatenate(y_rows[p], axis=0) for p in pairs]
    mean = [_segsum(x, ones_bd) * (1.0 / RWKV_HEAD) for x in y]
    yc = [y[p] - mean[p] for p in pairs]
    var = [_segsum(x * x, ones_bd) * (1.0 / RWKV_HEAD) for x in yc]
    bonus_dot = [_segsum(r[p] * kp[p] * rkw_ref[:, lanes_of[p]], ones_bd) for p in pairs]
    for p in pairs:
        sl = lanes_of[p]
        yn = yc[p] * lax.rsqrt(var[p] + GN_EPS) * lnw_ref[:, sl] + lnb_ref[:, sl]
        o_ref[:, sl] = (yn + bonus_dot[p] * v[p]) * g_ref[:, sl]


def _wkv_step(rkv, logd, a, g, kkw, kaw, rkw, lnw, lnb, s0):
    batch, d = logd.shape
    n = STEP_SEQS
    tok = lambda which: pl.BlockSpec((None, n, d), lambda i, which=which: (which, i, 0))
    tok2 = pl.BlockSpec((n, d), lambda i: (i, 0))
    vec = pl.BlockSpec((1, d), lambda i: (0, 0))
    state = pl.BlockSpec((n, RWKV_HEADS, RWKV_HEAD, RWKV_HEAD), lambda i: (i, 0, 0, 0))
    return pl.pallas_call(
        _wkv_step_body,
        grid=(batch // n,),
        in_specs=[tok(0), tok(1), tok(2), tok2, tok2, tok2, vec, vec, vec, vec, vec, state],
        out_specs=[tok2, state],
        out_shape=[jax.ShapeDtypeStruct((batch, d), F32),
                   jax.ShapeDtypeStruct((batch, RWKV_HEADS, RWKV_HEAD, RWKV_HEAD), F32)],
        compiler_params=_params("parallel"),
        name="wkv_step",
    )(rkv, rkv, rkv, logd, a, g, kkw, kaw, rkw, lnw, lnb, s0)


WKV_CHUNK = 64
WKV_STEP_PAIRS = 16
WKV_STEP_LANES = WKV_STEP_PAIRS * LANES


def _dot_nt(a, b):
    return lax.dot_general(a, b, (((1,), (1,)), ((), ())), preferred_element_type=F32)


def _split3(x):
    hi = x.astype(BF16)
    r1 = x - hi.astype(F32)
    mid = r1.astype(BF16)
    lo = (r1 - mid.astype(F32)).astype(BF16)
    return hi, mid, lo


def _wkv_chunk_body(r_ref, k_ref, v_ref, ld_ref, a_ref, g_ref, kkw_ref, kaw_ref, rkw_ref, lnw_ref, lnb_ref,
                    o_ref, st_ref, s_scr):
    t_chunk = pl.program_id(2)
    c = WKV_CHUNK
    ones_bd = _block_ones(LANES, RWKV_HEAD)
    row = lax.broadcasted_iota(jnp.int32, (LANES, LANES), 0)
    col = lax.broadcasted_iota(jnp.int32, (LANES, LANES), 1)
    strict = col < row
    incl = col <= row
    eye = jnp.where(row == col, 1.0, 0.0)
    ltri = jnp.where(incl, 1.0, 0.0).astype(BF16)
    head0 = lax.broadcasted_iota(jnp.int32, (c, LANES), 1) < RWKV_HEAD
    stack = lambda x: jnp.concatenate([jnp.where(head0, x, 0.0), jnp.where(head0, 0.0, x)], axis=0)

    @pl.when(t_chunk == 0)
    def _zero_state():
        s_scr[...] = jnp.zeros_like(s_scr)

    pairs = range(WKV_STEP_PAIRS)
    lanes_of = [slice(pp * LANES, (pp + 1) * LANES) for pp in pairs]
    each = lambda fn, *lists: [fn(*args) for args in zip(*lists)]
    bf = lambda x: x.astype(BF16)
    r = [r_ref[0, :, sl] for sl in lanes_of]
    k = [k_ref[0, :, sl] for sl in lanes_of]
    v = [v_ref[0, :, sl] for sl in lanes_of]
    ld = [ld_ref[0, :, sl] for sl in lanes_of]
    a = [a_ref[0, :, sl] for sl in lanes_of]
    kk_raw = each(lambda x, sl: x * kkw_ref[:, sl], k, lanes_of)
    segsum = lambda x: _dot(bf(x), ones_bd)
    kk_norm = each(lambda x: jnp.sqrt(segsum(x * x)), kk_raw)
    kk = each(lambda x, n: x / jnp.maximum(n, 1e-12), kk_raw, kk_norm)
    kp = each(lambda x, y, sl: x * (1.0 + (y - 1.0) * kaw_ref[:, sl]), k, a, lanes_of)
    ka = each(lambda x, y: x * y, kk, a)
    ld_pieces = each(lambda x: _split3(jnp.concatenate([x, jnp.zeros_like(x)], axis=0))[:2], ld)
    cum2 = [_dot(ltri, hi) + _dot(ltri, lo) for hi, lo in ld_pieces]
    cum = [x[:c] for x in cum2]
    gam_end = [jnp.exp(x[c:]) for x in cum2]
    gam_inv = [jnp.exp(-x) for x in cum]
    at = each(lambda x, y: x * y, ka, gam_inv)
    kt = each(lambda x, y: x * y, kp, gam_inv)
    x_st = each(lambda kk_, r_, cum_, ld_: bf(jnp.concatenate(
        [stack(kk_ * jnp.exp(cum_ - ld_)), stack(r_ * jnp.exp(cum_))], axis=0)), kk, r, cum, ld)
    y_st = each(lambda x, y: bf(jnp.concatenate([stack(x), stack(y)], axis=0)), at, kt)
    gram = each(_dot_nt, x_st, y_st)
    m_a = [jnp.where(strict, x[:LANES, :LANES], 0.0) for x in gram]
    m_k = [bf(jnp.where(strict, x[:LANES, LANES:], 0.0)) for x in gram]
    n_a = [bf(jnp.where(incl, x[LANES:, :LANES], 0.0)) for x in gram]
    n_k = [bf(jnp.where(incl, x[LANES:, LANES:], 0.0)) for x in gram]
    power = [-x for x in m_a]
    inv = [eye + x for x in power]
    for _ in range(5):
        power = [_dot(bf(x), bf(x)) for x in power]
        inv = each(lambda t, p: t + _dot(bf(t), bf(p)), inv, power)
    inv = [bf(x) for x in inv]
    v_st = [stack(x) for x in v]
    v_stb = [bf(x) for x in v_st]
    v_stt = [bf(x.T) for x in v_st]
    at_end = each(lambda x, y: bf(stack(x * y)), at, gam_end)
    kt_end = each(lambda x, y: bf(stack(x * y)), kt, gam_end)
    s_from_v = each(_dot, v_stt, kt_end)
    m_kv = each(_dot, m_k, v_stb)
    n_kv = each(_dot, n_k, v_stb)
    s = [s_scr[pp] for pp in pairs]
    proj = each(lambda x, s_: _dot_nt(x, bf(s_)), x_st, s)
    u = each(lambda t, p, mv: -_dot(t, bf(p[:LANES] + mv)), inv, proj, m_kv)
    y2 = each(lambda p, n, u_, nv: p[LANES:] + _dot(n, bf(u_)) + nv, proj, n_a, u, n_kv)
    s_from_u = each(lambda u_, x: _dot(bf(u_.T), x), u, at_end)
    for pp in pairs:
        s_scr[pp] = s[pp] * jnp.concatenate([gam_end[pp], gam_end[pp]], axis=0) + s_from_u[pp] + s_from_v[pp]
    y = [x[:c] + x[c:] for x in y2]
    mean = [segsum(x) * (1.0 / RWKV_HEAD) for x in y]
    yc = each(lambda x, m: x - m, y, mean)
    var = [segsum(x * x) * (1.0 / RWKV_HEAD) for x in yc]
    bonus_dot = each(lambda r_, kp_, sl: segsum(r_ * kp_ * rkw_ref[:, sl]), r, kp, lanes_of)
    for pp in pairs:
        sl = lanes_of[pp]
        yn = yc[pp] * lax.rsqrt(var[pp] + GN_EPS) * lnw_ref[:, sl] + lnb_ref[:, sl]
        o_ref[0, :, sl] = (yn + bonus_dot[pp] * v[pp]) * g_ref[0, :, sl]

    @pl.when(t_chunk == pl.num_programs(2) - 1)
    def _store_state():
        for pp in range(WKV_STEP_PAIRS):
            s = s_scr[pp]
            st_ref[0, 2 * pp] = s[:RWKV_HEAD, :RWKV_HEAD]
            st_ref[0, 2 * pp + 1] = s[RWKV_HEAD:, RWKV_HEAD:]


def _wkv_chunked(rkv, logd, a, g, kkw, kaw, rkw, lnw, lnb, batch, seq_len):
    d = D_MODEL
    c = WKV_CHUNK
    w = WKV_STEP_LANES
    rkv4 = rkv.reshape(3, batch, seq_len, d)
    seq3 = lambda x: x.reshape(batch, seq_len, d)
    tok = lambda which: pl.BlockSpec((None, 1, c, w), lambda b, p, t, which=which: (which, b, t, p))
    tok3 = pl.BlockSpec((1, c, w), lambda b, p, t: (b, t, p))
    vec = pl.BlockSpec((1, w), lambda b, p, t: (0, p))
    state = pl.BlockSpec((1, 2 * WKV_STEP_PAIRS, RWKV_HEAD, RWKV_HEAD), lambda b, p, t: (b, p, 0, 0))
    o, st = pl.pallas_call(
        _wkv_chunk_body,
        grid=(batch, d // w, seq_len // c),
        in_specs=[tok(0), tok(1), tok(2), tok3, tok3, tok3, vec, vec, vec, vec, vec],
        out_specs=[tok3, state],
        out_shape=[jax.ShapeDtypeStruct((batch, seq_len, d), F32),
                   jax.ShapeDtypeStruct((batch, RWKV_HEADS, RWKV_HEAD, RWKV_HEAD), F32)],
        scratch_shapes=[pltpu.VMEM((WKV_STEP_PAIRS, LANES, LANES), F32)],
        compiler_params=_params("parallel", "parallel", "arbitrary"),
        name="wkv_chunked",
    )(rkv4, rkv4, rkv4, seq3(logd), seq3(a), seq3(g), kkw, kaw, rkw, lnw, lnb)
    return o.reshape(batch * seq_len, d), st


SB_STEP_HEADS = 8
SB_STEP_LANES = SB_STEP_HEADS * SB_HEAD_DIM


def _sb_prompt_body(q_ref, k_ref, v_ref, b_ref, o_ref, acc_scr, tail_scr):
    qi = pl.program_id(2)
    scale = SB_HEAD_DIM ** -0.5
    t_idx = lax.broadcasted_iota(jnp.int32, (Q_BLOCK, Q_BLOCK), 0)
    s_idx = lax.broadcasted_iota(jnp.int32, (Q_BLOCK, Q_BLOCK), 1)
    later = jnp.where(t_idx > s_idx, 1.0, 0.0).astype(BF16)
    later_and_ones = jnp.concatenate([later, jnp.ones((Q_BLOCK, Q_BLOCK), BF16)], axis=1)

    heads = range(SB_STEP_HEADS)
    lanes_of = [slice(hh * SB_HEAD_DIM, (hh + 1) * SB_HEAD_DIM) for hh in heads]

    def visit(kb, diagonal):
        start = pl.multiple_of(kb * Q_BLOCK, Q_BLOCK)
        keep = (lambda x: jnp.where(s_idx < t_idx, x, 0.0)) if diagonal else (lambda x: x)
        z = [_dot_nt(q_ref[0, :, sl].astype(BF16), k_ref[0, pl.ds(start, Q_BLOCK), sl].astype(BF16)) * scale
             + b_ref[:, sl] for sl in lanes_of]
        sp = [_softplus(x) for x in z]
        log_not = [keep(-x).astype(BF16) for x in sp]
        sums = [_dot(x, later_and_ones) for x in log_not]
        tail = [0.0 if diagonal else tail_scr[hh] for hh in heads]
        p = [keep(jnp.exp(z[hh] - sp[hh] + sums[hh][:, :Q_BLOCK] + tail[hh])).astype(BF16) for hh in heads]
        pv = [_dot(p[hh], v_ref[0, pl.ds(start, Q_BLOCK), lanes_of[hh]].astype(BF16)) for hh in heads]
        for hh in heads:
            acc_scr[hh] = pv[hh] if diagonal else acc_scr[hh] + pv[hh]
            tail_scr[hh] = tail[hh] + sums[hh][:, Q_BLOCK:]

    visit(qi, True)

    def body(j, carry):
        visit(qi - j, False)
        return carry

    lax.fori_loop(1, qi + 1, body, 0)
    for hh in heads:
        o_ref[0, :, hh * SB_HEAD_DIM:(hh + 1) * SB_HEAD_DIM] = acc_scr[hh]


def _sb_prompt(q, k, v, bias_cols, batch, seq_len):
    d = D_MODEL
    w = SB_STEP_LANES
    seq3 = lambda x: x.reshape(batch, seq_len, d)
    qspec = pl.BlockSpec((1, Q_BLOCK, w), lambda b, h, i: (b, i, h))
    kspec = pl.BlockSpec((1, seq_len, w), lambda b, h, i: (b, 0, h))
    vspec = kspec
    block_state = pltpu.VMEM((SB_STEP_HEADS, Q_BLOCK, SB_HEAD_DIM), F32)
    out = pl.pallas_call(
        _sb_prompt_body,
        grid=(batch, d // w, seq_len // Q_BLOCK),
        in_specs=[qspec, kspec, vspec, pl.BlockSpec((1, w), lambda b, h, i: (0, h))],
        out_specs=pl.BlockSpec((1, Q_BLOCK, w), lambda b, h, i: (b, i, h)),
        out_shape=jax.ShapeDtypeStruct((batch, seq_len, d), F32),
        scratch_shapes=[block_state, block_state],
        compiler_params=_params("parallel", "parallel", "arbitrary"),
        name="sb_prompt",
    )(seq3(q), seq3(k), seq3(v), bias_cols)
    return out.reshape(batch * seq_len, d)


PAGE_ROWS = PAGE_SIZE * SB_HEADS


SB_STEP_SEQS = 8


def _sb_sample_body(pt_ref, q_ref, kn_ref, vn_ref, *refs):
    n = SB_STEP_SEQS
    kc_refs, vc_refs = refs[:n], refs[n:2 * n]
    b_ref, col_ref, exp_ref, o_ref, acc_scr, tail_scr = refs[2 * n:]
    seqs = range(n)
    step = pl.program_id(1)
    n_steps = pl.num_programs(1)
    scale = SB_HEAD_DIM ** -0.5
    bias = b_ref[...]
    head = lax.broadcasted_iota(jnp.int32, (SB_HEADS, PAGE_ROWS), 0)
    own_head = lax.broadcasted_iota(jnp.int32, (SB_HEADS, PAGE_ROWS), 1) % SB_HEADS == head
    r_idx = lax.broadcasted_iota(jnp.int32, (PAGE_SIZE, PAGE_SIZE), 0)
    c_idx = lax.broadcasted_iota(jnp.int32, (PAGE_SIZE, PAGE_SIZE), 1)
    later = jnp.where(r_idx > c_idx, 1.0, 0.0).astype(BF16)
    ones = jnp.ones((PAGE_SIZE, PAGE_SIZE), BF16)

    @pl.when(step == 0)
    def _init():
        n_past = n_steps * PAGE_SIZE
        new_is_before = jnp.full((SB_HEADS, LANES), n_past, jnp.int32) < n_past
        for i in seqs:
            z_new = jnp.sum(q_ref[i] * kn_ref[i], axis=-1, keepdims=True) * scale + bias
            sp_new = _softplus(z_new)
            tail_scr[i] = jnp.where(new_is_before, -sp_new, 0.0)
            acc_scr[i] = jnp.where(new_is_before, jnp.exp(z_new - sp_new), 0.0) * vn_ref[i]

    z_all = [_dot_nt(q_ref[i].astype(BF16), kc_refs[i][0].astype(BF16)) for i in seqs]
    z_own = [_split3(jnp.where(own_head, x, 0.0)) for x in z_all]
    collapse = col_ref[...]
    z = [(_dot(p3[0], collapse) + _dot(p3[1], collapse) + _dot(p3[2], collapse)) * scale + bias for p3 in z_own]
    sp = [_softplus(x) for x in z]
    log_not = [(-x).astype(BF16) for x in sp]
    after = [_dot(x, later) for x in log_not]
    total = [_dot(x, ones) for x in log_not]
    tail = [tail_scr[i] for i in seqs]
    p = [jnp.exp(z[i] - sp[i] + after[i] + tail[i]).astype(BF16) for i in seqs]
    p_rows = [jnp.where(own_head, _dot(x, exp_ref[...]), 0.0).astype(BF16) for x in p]
    pv = [_dot(p_rows[i], vc_refs[i][0].astype(BF16)) for i in seqs]
    for i in seqs:
        acc_scr[i] += pv[i]
        tail_scr[i] = tail[i] + total[i]

    @pl.when(step == n_steps - 1)
    def _finish():
        o_ref[...] = acc_scr[...]


def _sb_sample(q, k_new, v_new, cache_k, cache_v, page_table, bias):
    batch, d = q.shape
    n_pages = page_table.shape[1]
    n_phys = cache_k.shape[0]
    pages = lambda c: c.reshape(n_phys, PAGE_ROWS, SB_HEAD_DIM)
    row_token = lax.broadcasted_iota(jnp.int32, (PAGE_ROWS, PAGE_SIZE), 0) // SB_HEADS
    collapse = (row_token == lax.broadcasted_iota(jnp.int32, (PAGE_ROWS, PAGE_SIZE), 1)).astype(BF16)
    bias_lanes = jnp.broadcast_to(bias[:, None], (SB_HEADS, LANES)).astype(F32)
    n = SB_STEP_SEQS
    tok = pl.BlockSpec((n, SB_HEADS, SB_HEAD_DIM), lambda b, s, pt: (b, 0, 0))
    page = lambda i: pl.BlockSpec((1, PAGE_ROWS, SB_HEAD_DIM),
                                  lambda b, s, pt, i=i: (pt[b * n + i, n_pages - 1 - s], 0, 0))
    const = lambda shape: pl.BlockSpec(shape, lambda b, s, pt: (0, 0))
    heads3 = lambda x: x.reshape(batch, SB_HEADS, SB_HEAD_DIM)
    per_seq = pltpu.VMEM((n, SB_HEADS, SB_HEAD_DIM), F32)
    out = pl.pallas_call(
        _sb_sample_body,
        grid_spec=pltpu.PrefetchScalarGridSpec(
            num_scalar_prefetch=1,
            grid=(batch // n, n_pages),
            in_specs=[tok, tok, tok] + [page(i) for i in range(n)] * 2 + [
                const((SB_HEADS, LANES)), const((PAGE_ROWS, PAGE_SIZE)), const((PAGE_SIZE, PAGE_ROWS))],
            out_specs=tok,
            scratch_shapes=[per_seq, per_seq],
        ),
        out_shape=jax.ShapeDtypeStruct((batch, SB_HEADS, SB_HEAD_DIM), F32),
        compiler_params=_params("parallel", "arbitrary"),
        name="sb_sample",
    )(page_table, heads3(q), heads3(k_new), heads3(v_new), *([pages(cache_k)] * n), *([pages(cache_v)] * n),
      bias_lanes, collapse, collapse.T)
    return out.reshape(batch, d)


ROUTER_OFFSET = N_GROUPS


def _router_body(x_ref, w_ref, b_ref, gate_ref, route_ref, count_ref, count_scr):
    logits = jnp.dot(x_ref[...], w_ref[...], preferred_element_type=F32,
                     precision=lax.Precision.HIGHEST) + b_ref[...]
    lane = lax.broadcasted_iota(jnp.int32, logits.shape, 1)
    neg = -jnp.inf
    first_lane_of = lambda hit: jnp.min(jnp.where(hit, lane, LANES), axis=-1, keepdims=True)
    is_group = lane < N_GROUPS
    glog = jnp.where(is_group, logits, neg)
    gmax = jnp.max(glog, axis=-1, keepdims=True)
    g_sel = first_lane_of(glog == gmax)
    p_group = 1.0 / jnp.sum(jnp.exp(glog - gmax), axis=-1, keepdims=True)
    in_group = ((lane - ROUTER_OFFSET) // EXPERTS_PER_GROUP == g_sel) & (lane >= ROUTER_OFFSET) \
        & (lane < ROUTER_OFFSET + N_EXPERTS)
    elog = jnp.where(in_group, logits, neg)
    top1 = jnp.max(elog, axis=-1, keepdims=True)
    lane1 = first_lane_of(elog == top1)
    rest = jnp.where(lane == lane1, neg, elog)
    top2 = jnp.max(rest, axis=-1, keepdims=True)
    lane2 = first_lane_of(rest == top2)
    e2 = jnp.exp(top2 - top1)
    w1 = p_group / (1.0 + e2)
    w2 = p_group * e2 / (1.0 + e2)
    gate_ref[...] = jnp.where(lane == lane1, w1, 0.0) + jnp.where(lane == lane2, w2, 0.0)
    @pl.when(pl.program_id(0) == 0)
    def _zero_counts():
        count_scr[...] = jnp.zeros_like(count_scr)

    tm = logits.shape[0]
    earlier = (lax.broadcasted_iota(jnp.int32, (tm, tm), 1)
               < lax.broadcasted_iota(jnp.int32, (tm, tm), 0)).astype(BF16)
    hot1 = jnp.where(lane == lane1, 1.0, 0.0)
    hot2 = jnp.where(lane == lane2, 1.0, 0.0)
    seen = count_scr[0:1, :]
    total1 = jnp.sum(hot1, axis=0, keepdims=True)
    rank1 = jnp.sum(hot1 * (seen + _dot(earlier, hot1.astype(BF16))), axis=-1, keepdims=True)
    rank2 = jnp.sum(hot2 * (seen + total1 + _dot(earlier, hot2.astype(BF16))), axis=-1, keepdims=True)
    seen = seen + total1 + jnp.sum(hot2, axis=0, keepdims=True)
    count_scr[...] = jnp.broadcast_to(seen, count_scr.shape)
    count_ref[...] = jnp.broadcast_to(seen, count_ref.shape)
    route_ref[...] = (jnp.where(lane == 0, (lane1 - ROUTER_OFFSET).astype(F32), 0.0)
                      + jnp.where(lane == 1, (lane2 - ROUTER_OFFSET).astype(F32), 0.0)
                      + jnp.where(lane == 2, w1, 0.0) + jnp.where(lane == 3, w2, 0.0)
                      + jnp.where(lane == 4, rank1, 0.0) + jnp.where(lane == 5, rank2, 0.0))


def _router(x, w_router, b_router, tm):
    n, d = x.shape
    out = jax.ShapeDtypeStruct((n, LANES), F32)
    row = pl.BlockSpec((tm, LANES), lambda i: (i, 0))
    counts = pl.BlockSpec((SUBLANES, LANES), lambda i: (0, 0))
    return pl.pallas_call(
        _router_body,
        grid=(n // tm,),
        in_specs=[pl.BlockSpec((tm, d), lambda i: (i, 0)),
                  pl.BlockSpec((d, LANES), lambda i: (0, 0)),
                  pl.BlockSpec((1, LANES), lambda i: (0, 0))],
        out_specs=[row, row, counts],
        out_shape=[out, out, jax.ShapeDtypeStruct((SUBLANES, LANES), F32)],
        scratch_shapes=[pltpu.VMEM((SUBLANES, LANES), F32)],
        compiler_params=_params("arbitrary"),
        name="moe_router",
    )(x, w_router, b_router)


def _moe_body(x_ref, gate_ref, res_ref, wg_ref, wu_ref, wd_ref, o_ref, xb_scr):
    e = pl.program_id(1)

    @pl.when(e == 0)
    def _init():
        xb_scr[...] = x_ref[...].astype(BF16)
        o_ref[...] = res_ref[...]

    xb = xb_scr[...]
    hg = _dot(xb, wg_ref[0].astype(BF16))
    hu = _dot(xb, wu_ref[0].astype(BF16))
    gate = gate_ref[...]
    lane = lax.broadcasted_iota(jnp.int32, gate.shape, 1)
    ge = jnp.sum(jnp.where(lane == e + ROUTER_OFFSET, gate, 0.0), axis=-1, keepdims=True)
    act = (hg * _sigmoid(hg)) * hu * ge
    o_ref[...] += _dot(act.astype(BF16), wd_ref[0].astype(BF16))


def _moe_dense(x, gate, residual, wg, wu, wd, layer, tm):
    n, d = x.shape
    row = lambda i, e: (i, 0)
    return pl.pallas_call(
        _moe_body,
        grid=(n // tm, N_EXPERTS),
        in_specs=[pl.BlockSpec((tm, d), row), pl.BlockSpec((tm, LANES), row), pl.BlockSpec((tm, d), row),
                  pl.BlockSpec((None, 1, d, D_EXPERT), lambda i, e: (layer, e, 0, 0)),
                  pl.BlockSpec((None, 1, d, D_EXPERT), lambda i, e: (layer, e, 0, 0)),
                  pl.BlockSpec((None, 1, D_EXPERT, d), lambda i, e: (layer, e, 0, 0))],
        out_specs=pl.BlockSpec((tm, d), row),
        out_shape=jax.ShapeDtypeStruct((n, d), F32),
        scratch_shapes=[pltpu.VMEM((tm, d), BF16)],
        compiler_params=_params("parallel", "arbitrary"),
        name="moe_experts",
    )(x, gate, residual, wg, wu, wd)


MOE_TILE = 256
COMBINE_TILE = 256


def _row_copy(src_hbm, index_ref, base, buf, sem):
    def copy(r):
        return pltpu.make_async_copy(src_hbm.at[pl.ds(index_ref[base + r], 1)], buf.at[pl.ds(r, 1)], sem)
    return copy


def _for_rows(n_rows, fn):
    def body(r, carry):
        fn(r)
        return carry
    lax.fori_loop(0, n_rows, body, 0, unroll=8)


def _gather_pipeline(step, n_steps, copies_of, n_rows):
    slot = step % 2

    @pl.when(step == 0)
    def _prime():
        _for_rows(n_rows, lambda r: copies_of(0, 0)(r).start())

    @pl.when(step + 1 < n_steps)
    def _prefetch():
        _for_rows(n_rows, lambda r: copies_of(step + 1, 1 - slot)(r).start())

    _for_rows(n_rows, lambda r: copies_of(step, slot)(r).wait())
    return slot


def _moe_dispatch_body(pos_ref, h_ref, zeros_hbm, xs_hbm, sem):
    del zeros_hbm
    i = pl.program_id(0)
    tm = h_ref.shape[0]

    def copy(j):
        r = j % tm
        return pltpu.make_async_copy(h_ref.at[pl.ds(r, 1)], xs_hbm.at[pl.ds(pos_ref[i * 2 * tm + j], 1)], sem)

    _for_rows(2 * tm, lambda j: copy(j).start())
    _for_rows(2 * tm, lambda j: copy(j).wait())


def _moe_sparse_body(tile_expert_ref, n_used_ref, x_ref, wg_ref, wu_ref, wd_ref, y_ref, wg_bf, wu_bf, wd_bf):
    t = pl.program_id(0)

    @pl.when(jnp.logical_or(t == 0, tile_expert_ref[t] != tile_expert_ref[jnp.maximum(t - 1, 0)]))
    def _new_expert():
        wg_bf[...] = wg_ref[0].astype(BF16)
        wu_bf[...] = wu_ref[0].astype(BF16)
        wd_bf[...] = wd_ref[0].astype(BF16)

    @pl.when(t < n_used_ref[0])
    def _compute():
        x = x_ref[...].astype(BF16)
        hg = _dot(x, wg_bf[...])
        hu = _dot(x, wu_bf[...])
        y_ref[...] = _dot(((hg * _sigmoid(hg)) * hu).astype(BF16), wd_bf[...])

    @pl.when(t >= n_used_ref[0])
    def _unused_tile():
        y_ref[...] = jnp.zeros_like(y_ref)


def _moe_combine_body(pos_ref, x_ref, route_ref, y_hbm, *rest, normed):
    g_ref = rest[0] if normed else None
    o_ref, ybuf, sem = rest[-3:]
    i = pl.program_id(0)
    tm = x_ref.shape[0]
    copies_of = lambda tile, s: _row_copy(y_hbm, pos_ref, tile * 2 * tm, ybuf.at[s], sem.at[s])
    slot = _gather_pipeline(i, pl.num_programs(0), copies_of, 2 * tm)
    rows = ybuf[slot]
    route = route_ref[...]
    x = x_ref[...] + route[:, 2:3] * rows[:tm] + route[:, 3:4] * rows[tm:]
    if normed:
        x = x * lax.rsqrt(jnp.mean(x * x, axis=-1, keepdims=True) + RMS_EPS) * g_ref[...]
    o_ref[...] = x


def _route_plan(route, counts, tile, tm):
    n = route.shape[0]
    counts = counts[0, ROUTER_OFFSET:ROUTER_OFFSET + N_EXPERTS].astype(jnp.int32)
    padded = (counts + tile - 1) // tile * tile
    ends = jnp.cumsum(padded)
    offs = (ends - padded).astype(F32)
    expert = route[:, 0:2]
    hot = expert[:, :, None] == jnp.arange(N_EXPERTS, dtype=F32)[None, None, :]
    pos = (jnp.sum(jnp.where(hot, offs[None, None, :], 0.0), axis=-1) + route[:, 4:6]).astype(jnp.int32)
    n_tiles = (2 * n + N_EXPERTS * tile) // tile
    tile_start = jnp.arange(n_tiles, dtype=jnp.int32) * tile
    tile_expert = jnp.minimum(jnp.sum((ends[None, :] <= tile_start[:, None]).astype(jnp.int32), axis=1),
                              N_EXPERTS - 1)
    n_used = (ends[-1] // tile).astype(jnp.int32).reshape(1)
    pos_tiled = pos.reshape(n // tm, tm, 2).transpose(0, 2, 1).reshape(2 * n)
    return pos_tiled, tile_expert, n_used


def _moe_sparse(h, route, counts, residual, wg, wu, wd, layer, out_norm=None):
    n, d = h.shape
    tile, tm = MOE_TILE, COMBINE_TILE
    pos_tiled, tile_expert, n_used = _route_plan(route, counts, tile, tm)
    p_rows = 2 * n + N_EXPERTS * tile
    token_tile = pl.BlockSpec((tm, d), lambda i, pos: (i, 0))
    xs = pl.pallas_call(
        _moe_dispatch_body,
        grid_spec=pltpu.PrefetchScalarGridSpec(
            num_scalar_prefetch=1,
            grid=(n // tm,),
            in_specs=[token_tile, pl.BlockSpec(memory_space=pl.ANY)],
            out_specs=pl.BlockSpec(memory_space=pl.ANY),
            scratch_shapes=[pltpu.SemaphoreType.DMA(())],
        ),
        out_shape=jax.ShapeDtypeStruct((p_rows, d), F32),
        input_output_aliases={2: 0},
        compiler_params=_params("arbitrary"),
        name="moe_dispatch",
    )(pos_tiled, h, jnp.zeros((p_rows, d), F32))
    y = pl.pallas_call(
        _moe_sparse_body,
        grid_spec=pltpu.PrefetchScalarGridSpec(
            num_scalar_prefetch=2,
            grid=(p_rows // tile,),
            in_specs=[pl.BlockSpec((tile, d), lambda t, te, nu: (t, 0)),
                      pl.BlockSpec((None, 1, d, D_EXPERT), lambda t, te, nu: (layer, te[t], 0, 0)),
                      pl.BlockSpec((None, 1, d, D_EXPERT), lambda t, te, nu: (layer, te[t], 0, 0)),
                      pl.BlockSpec((None, 1, D_EXPERT, d), lambda t, te, nu: (layer, te[t], 0, 0))],
            out_specs=pl.BlockSpec((tile, d), lambda t, te, nu: (t, 0)),
            scratch_shapes=[pltpu.VMEM((d, D_EXPERT), BF16), pltpu.VMEM((d, D_EXPERT), BF16),
                            pltpu.VMEM((D_EXPERT, d), BF16)],
        ),
        out_shape=jax.ShapeDtypeStruct((p_rows, d), F32),
        compiler_params=_params("arbitrary"),
        name="moe_sparse_experts",
    )(tile_expert, n_used, xs, wg, wu, wd)
    normed = out_norm is not None
    norm_specs = [pl.BlockSpec((1, d), lambda i, pos: (0, 0))] if normed else []
    norm_args = [out_norm.reshape(1, d)] if normed else []
    return pl.pallas_call(
        functools.partial(_moe_combine_body, normed=normed),
        grid_spec=pltpu.PrefetchScalarGridSpec(
            num_scalar_prefetch=1,
            grid=(n // tm,),
            in_specs=[token_tile, pl.BlockSpec((tm, LANES), lambda i, pos: (i, 0)),
                      pl.BlockSpec(memory_space=pl.ANY)] + norm_specs,
            out_specs=token_tile,
            scratch_shapes=[pltpu.VMEM((2, 2 * tm, d), F32), pltpu.SemaphoreType.DMA((2,))],
        ),
        out_shape=jax.ShapeDtypeStruct((n, d), F32),
        compiler_params=_params("arbitrary"),
        name="moe_combine",
    )(pos_tiled, residual, route, y, *norm_args)


def _row_tile(n):
    return 256 if n % 256 == 0 else LANES


def _ffn(x, norm_g, w_router, b_router, wg, wu, wd, layer, out_norm=None):
    n = x.shape[0]
    tm = _row_tile(n)
    h = _rmsnorm(x, norm_g, tm)
    gate, route, counts = _router(h, w_router, b_router, tm)
    if n % COMBINE_TILE == 0 and n >= N_EXPERTS * MOE_TILE:
        return _moe_sparse(h, route, counts, x, wg, wu, wd, layer, out_norm)
    out = _moe_dense(h, gate, x, wg, wu, wd, layer, LANES)
    return out if out_norm is None else _rmsnorm(out, out_norm, tm)


def kernel(x_prompt, x_sample, state_tshift, state_wkv, cache_k, cache_v, page_table, norm_mix, norm_ffn, norm_final, rwkv_mu, rwkv_w0, rwkv_w1, rwkv_w2, rwkv_a0, rwkv_a1, rwkv_a2, rwkv_g1, rwkv_g2, rwkv_kk, rwkv_ka, rwkv_rk, rwkv_wr, rwkv_wk, rwkv_wv, rwkv_wo, rwkv_lnw, rwkv_lnb, sb_wqkv, sb_wo, sb_bias, moe_wgroup, moe_bgroup, moe_wexpert, moe_bexpert, moe_wgate, moe_wup, moe_wdown):
    d = D_MODEL
    bp, tp, _ = x_prompt.shape
    bs, ts, _ = x_sample.shape
    xp = x_prompt.reshape(bp * tp, d)
    xs = x_sample.reshape(bs * ts, d)
    row1 = lambda v: v.reshape(1, d)
    pad_cols = lambda w: jnp.pad(w, ((0, 0), (0, LORA_PAD - w.shape[1]))).astype(BF16)
    pad_rows = lambda w: jnp.pad(w, ((0, LORA_PAD - w.shape[0]), (0, 0))).astype(BF16)

    mu = rwkv_mu[0]
    mu_rkv = jnp.stack([mu[0], mu[2], mu[3]]).reshape(3, 1, d)
    mu_lora = jnp.stack([mu[1], mu[4], mu[5]]).reshape(3, 1, d)
    w_rkv = jnp.stack([rwkv_wr[0], rwkv_wk[0], rwkv_wv[0]]).astype(BF16)
    lora = (row1(rwkv_w0[0]), row1(rwkv_a0[0]), pad_cols(rwkv_w1[0]), pad_rows(rwkv_w2[0]),
            pad_cols(rwkv_a1[0]), pad_rows(rwkv_a2[0]), rwkv_g1[0].astype(BF16), rwkv_g2[0].astype(BF16))
    head_vecs = (row1(rwkv_kk[0]), row1(rwkv_ka[0]), row1(rwkv_rk[0]), row1(rwkv_lnw[0]), row1(rwkv_lnb[0]))
    wo = rwkv_wo[0].astype(BF16)

    def rwkv_layer(x, prev, s0, batch, seq_len):
        tm = _row_tile(x.shape[0])
        h = _rmsnorm(x, norm_mix[0], tm)
        rkv = _rwkv_rkv(h, prev, mu_rkv, w_rkv, tm, seq_len)
        logd, a, g = _rwkv_lora(h, prev, mu_lora, *lora, tm, seq_len)
        if s0 is None:
            o, s_final = _wkv_chunked(rkv, logd, a, g, *head_vecs, batch, seq_len)
        else:
            assert seq_len == 1, "a given initial state is only supported for single-token sequences"
            o, s_final = _wkv_step(rkv, logd, a, g, *head_vecs, s0)
        return _matmul(o, wo, tm, residual=x), h, s_final

    xp, hp, wkv_p = rwkv_layer(xp, None, None, bp, tp)
    xs, hs, wkv_s = rwkv_layer(xs, state_tshift[0], state_wkv[0], bs, ts)
    shift_p = hp.reshape(bp, tp, d)[:, -1]
    shift_s = hs.reshape(bs, ts, d)[:, -1]

    def ffn(x, i, out_norm=None):
        w_router = jnp.pad(jnp.concatenate([moe_wgroup[i], moe_wexpert[i]], axis=1),
                           ((0, 0), (0, LANES - N_GROUPS - N_EXPERTS)))
        b_router = jnp.pad(jnp.concatenate([moe_bgroup[i], moe_bexpert[i]]),
                           (0, LANES - N_GROUPS - N_EXPERTS)).reshape(1, LANES)
        return _ffn(x, norm_ffn[i], w_router, b_router, moe_wgate, moe_wup, moe_wdown, i, out_norm)

    xp = ffn(xp, 0)
    xs = ffn(xs, 0)

    w_qkv = jnp.stack(jnp.split(sb_wqkv[0], 3, axis=-1)).astype(BF16)
    w_att_out = sb_wo[0].astype(BF16)
    bias_cols = jnp.repeat(sb_bias[0].astype(F32), SB_HEAD_DIM).reshape(1, d)

    def qkv_proj(x):
        tm = _row_tile(x.shape[0])
        h = _rmsnorm(x, norm_mix[1], tm)
        return [_matmul(h, w_qkv[j], tm) for j in range(3)], tm

    qkv_p, tm_p = qkv_proj(xp)
    att_p = _sb_prompt(*qkv_p, bias_cols, bp, tp)
    xp = _matmul(att_p, w_att_out, tm_p, residual=xp)

    qkv_s, tm_s = qkv_proj(xs)
    att_s = _sb_sample(qkv_s[0], qkv_s[1], qkv_s[2], cache_k[0], cache_v[0], page_table, sb_bias[0])
    xs = _matmul(att_s, w_att_out, tm_s, residual=xs)

    y_prompt = ffn(xp, 1, norm_final).reshape(bp, tp, d)
    y_sample = ffn(xs, 1, norm_final).reshape(bs, ts, d)
    heads5 = lambda x, b, t: x.reshape(1, b, t, SB_HEADS, SB_HEAD_DIM)
    return (y_prompt, y_sample,
            shift_p[None], wkv_p[None],
            heads5(qkv_p[1], bp, tp), heads5(qkv_p[2], bp, tp),
            shift_s[None], wkv_s[None],
            heads5(qkv_s[1], bs, ts), heads5(qkv_s[2], bs, ts))
```
